```python
import math
import jax
import jax.numpy as jnp
from jax import lax
import numpy as np

D_MODEL = 1024
BATCH = 4
SEQ = 4096
DEPTH = 4
DEC_BATCH = 32
DEC_SEQ = 64
PAST_LEN = 2048

CHUNK = 64
Q_BLOCK = 128
H_A = 8
DN = 64
DR = 32
DV = 64
Q_RANK = 256
KV_RANK = 128
ROPE_THETA = 10000.0
MLA_SCALE = 1.0 / math.sqrt(DN + DR)
H_B = 8
N_B = 64
D_B = H_B * N_B
W_RANK = 64
A_RANK = 64
G_RANK = 128
D_MIX = H_A * DV + D_B
D_SHIFT = 3 * D_B + W_RANK + A_RANK + G_RANK
D_IN = Q_RANK + KV_RANK + DR + D_SHIFT
N_MEM = 256
MEM_HEADS = 4
MEM_HD = D_MODEL // MEM_HEADS
D_FF = 2816
EPS = 1e-6
GN_EPS = 64e-5
NEG_INF = -1e30

kernel_name = 'hybrid_mla_rwkv7_macaron_stream_step'


def rmsnorm(x, g):
    xf = x.astype(jnp.float32)
    y = xf * lax.rsqrt(jnp.mean(xf * xf, axis=-1, keepdims=True) + EPS)
    return (y * g.astype(jnp.float32)).astype(x.dtype)


def swiglu_half(x, g, w_gate, w_up, w_down):
    h = rmsnorm(x, g)
    return x + 0.5 * ((jax.nn.silu(h @ w_gate) * (h @ w_up)) @ w_down)


def rope(x, pos):
    half = DR // 2
    inv = ROPE_THETA ** (-jnp.arange(half, dtype=jnp.float32) / half)
    ang = pos.astype(jnp.float32)[:, None] * inv[None, :]
    shp = (1, pos.shape[0]) + (1,) * (x.ndim - 3) + (half,)
    cos = jnp.cos(ang).reshape(shp)
    sin = jnp.sin(ang).reshape(shp)
    xf = x.astype(jnp.float32)
    x1, x2 = xf[..., :half], xf[..., half:]
    return jnp.concatenate([x1 * cos - x2 * sin, x1 * sin + x2 * cos], axis=-1).astype(x.dtype)


def mla_block(q_lat, q_rope, q_pos, c_kv, k_rope, k_pos):
    s = (jnp.einsum('bqhc,bkc->bhqk', q_lat, c_kv)
         + jnp.einsum('bqhr,bkr->bhqk', q_rope, k_rope)).astype(jnp.float32) * MLA_SCALE
    limit = (q_pos // CHUNK + 1) * CHUNK
    mask = k_pos[None, :] < limit[:, None]
    s = jnp.where(mask[None, None], s, NEG_INF)
    p = jax.nn.softmax(s, axis=-1).astype(c_kv.dtype)
    return jnp.einsum('bhqk,bkc->bqhc', p, c_kv)


def wkv_scan(r, w, k, v, kk, a, s0):
    def step(S, inp):
        r_t, w_t, k_t, v_t, kk_t, a_t = inp
        sa = jnp.einsum('bhvk,bhk->bhv', S, -kk_t)
        S = (S * w_t[:, :, None, :] + sa[..., None] * (kk_t * a_t)[:, :, None, :]
             + v_t[..., None] * k_t[:, :, None, :])
        return S, jnp.einsum('bhvk,bhk->bhv', S, r_t)
    tm = lambda z: jnp.moveaxis(z, 1, 0)
    S, y = lax.scan(step, s0.astype(jnp.float32), (tm(r), tm(w), tm(k), tm(v), tm(kk), tm(a)))
    return S, jnp.moveaxis(y, 0, 1)


def token_mixer(h, pos, k_pos, ckv_past, krope_past, wkv0, shift0, lp):
    f32 = jnp.float32
    b, t, _ = h.shape
    proj = h @ lp['w_in']
    c_q, c_kv, k_r, p_b = jnp.split(proj, [Q_RANK, Q_RANK + KV_RANK, Q_RANK + KV_RANK + DR], axis=-1)
    q = (rmsnorm(c_q, lp['q_norm']) @ lp['w_uq']).reshape(b, t, H_A, DN + DR)
    q_rope = rope(q[..., DN:], pos)
    q_lat = jnp.einsum('bthd,hcd->bthc', q[..., :DN], lp['w_uk'])
    c_kv = rmsnorm(c_kv, lp['kv_norm'])
    k_r = rope(k_r, pos)
    if ckv_past is None:
        ckv_all, kr_all = c_kv, k_r
    else:
        ckv_all = jnp.concatenate([ckv_past, c_kv], axis=1)
        kr_all = jnp.concatenate([krope_past, k_r], axis=1)
    if t > Q_BLOCK:
        nb = t // Q_BLOCK
        to_blocks = lambda z: jnp.moveaxis(z.reshape((b, nb, Q_BLOCK) + z.shape[2:]), 1, 0)
        o = lax.map(lambda args: mla_block(args[0], args[1], args[2], ckv_all, kr_all, k_pos),
                    (to_blocks(q_lat), to_blocks(q_rope), pos.reshape(nb, Q_BLOCK)))
        o_lat = jnp.moveaxis(o, 0, 1).reshape(b, t, H_A, KV_RANK)
    else:
        o_lat = mla_block(q_lat, q_rope, pos, ckv_all, kr_all, k_pos)
    y_a = jnp.einsum('bthc,hcv->bthv', o_lat, lp['w_uv']).reshape(b, t, H_A * DV)
    prev = jnp.concatenate([shift0, p_b[:, :-1]], axis=1)
    xs = p_b + lp['shift_mu'] * (prev - p_b)
    r, k, v, xw, xa, xg = jnp.split(
        xs, [D_B, 2 * D_B, 3 * D_B, 3 * D_B + W_RANK, 3 * D_B + W_RANK + A_RANK], axis=-1)
    z = (lp['w0'] + jnp.tanh(xw) @ lp['w_up']).astype(f32)
    decay = jnp.exp(-jnp.exp(-jax.nn.softplus(-z) - 0.5))
    a = jax.nn.sigmoid((lp['a0'] + xa @ lp['a_up']).astype(f32))
    g = jax.nn.sigmoid(xg) @ lp['g_up']
    heads = lambda u: u.astype(f32).reshape(b, t, H_B, N_B)
    r_h, k_h, v_h, a_h, w_h = heads(r), heads(k), heads(v), heads(a), heads(decay)
    kk = k_h * lp['k_k'].astype(f32).reshape(H_B, N_B)
    kk = kk * lax.rsqrt(jnp.sum(kk * kk, axis=-1, keepdims=True) + 1e-12)
    k_h = k_h * (1.0 + (a_h - 1.0) * lp['k_a'].astype(f32).reshape(H_B, N_B))
    wkv, y = wkv_scan(r_h, w_h, k_h, v_h, kk, a_h, wkv0)
    mu = jnp.mean(y, axis=-1, keepdims=True)
    var = jnp.mean(jnp.square(y - mu), axis=-1, keepdims=True)
    y = ((y - mu) * lax.rsqrt(var + GN_EPS) * lp['gn_gain'].astype(f32).reshape(H_B, N_B)
         + lp['gn_bias'].astype(f32).reshape(H_B, N_B))
    y = y + jnp.sum(r_h * k_h * lp['r_k'].astype(f32), axis=-1, keepdims=True) * v_h
    y_b = y.reshape(b, t, D_B).astype(h.dtype) * g
    out = jnp.concatenate([y_a, y_b], axis=-1) @ lp['w_out']
    return out, c_kv, k_r, wkv.astype(wkv0.dtype), p_b[:, -1:]


def mem_kv(mem, lp):
    b = mem.shape[0]
    m = rmsnorm(mem, lp['mem_kv_norm'])
    mk = (m @ lp['w_mk']).reshape(b, N_MEM, MEM_HEADS, MEM_HD)
    mv = (m @ lp['w_mv']).reshape(b, N_MEM, MEM_HEADS, MEM_HD)
    return mk, mv


def mem_attend(x, mk, mv, lp):
    b, t, _ = x.shape
    h = rmsnorm(x, lp['xattn_norm'])
    q = (h @ lp['w_mq']).reshape(b, t, MEM_HEADS, MEM_HD)
    s = jnp.einsum('bqhd,bmhd->bhqm', q, mk).astype(jnp.float32) / math.sqrt(MEM_HD)
    p = jax.nn.softmax(s, axis=-1).astype(x.dtype)
    o = jnp.einsum('bhqm,bmhd->bqhd', p, mv).reshape(b, t, D_MODEL)
    return x + o @ lp['w_mo']


def layer(x, pos, k_pos, ckv_past, krope_past, wkv0, shift0, mk, mv, lp):
    x = swiglu_half(x, lp['ffn1_norm'], lp['ffn1_w_gate'], lp['ffn1_w_up'], lp['ffn1_w_down'])
    y, c_kv, k_r, wkv, shift = token_mixer(rmsnorm(x, lp['mix_norm']), pos, k_pos,
                                           ckv_past, krope_past, wkv0, shift0, lp)
    x = x + y
    x = mem_attend(x, mk, mv, lp)
    x = swiglu_half(x, lp['ffn2_norm'], lp['ffn2_w_gate'], lp['ffn2_w_up'], lp['ffn2_w_down'])
    return x, c_kv, k_r, wkv, shift


def setup_inputs(seed: int = 0) -> dict:
    key = jax.random.key(seed)
    keys = jax.random.split(key, 64)
    ctr = [0]

    def nrm(shape, scale):
        kk = keys[ctr[0]]
        ctr[0] += 1
        return scale * jax.random.normal(kk, shape, jnp.float32)

    def gain(shape):
        return 1.0 + nrm(shape, 0.05)

    L, D = DEPTH, D_MODEL
    return {
        'x_prompt': nrm((BATCH, SEQ, D), 1.0),
        'x_sample': nrm((DEC_BATCH, DEC_SEQ, D), 1.0),
        'mem_prompt': nrm((BATCH, N_MEM, D), 1.0),
        'cache_ckv': nrm((L, DEC_BATCH, PAST_LEN, KV_RANK), 1.0),
        'cache_krope': nrm((L, DEC_BATCH, PAST_LEN, DR), 1.0),
        'cache_mem_k': nrm((L, DEC_BATCH, N_MEM, MEM_HEADS, MEM_HD), 1.0),
        'cache_mem_v': nrm((L, DEC_BATCH, N_MEM, MEM_HEADS, MEM_HD), 1.0),
        'state_wkv': nrm((L, DEC_BATCH, H_B, N_B, N_B), 0.5),
        'state_shift': nrm((L, DEC_BATCH, 1, D_SHIFT), 1.0),
        'ffn1_norm': gain((L, D)),
        'ffn1_w_gate': nrm((L, D, D_FF), D ** -0.5),
        'ffn1_w_up': nrm((L, D, D_FF), D ** -0.5),
        'ffn1_w_down': nrm((L, D_FF, D), D_FF ** -0.5),
        'mix_norm': gain((L, D)),
        'w_in': nrm((L, D, D_IN), D ** -0.5),
        'q_norm': gain((L, Q_RANK)),
        'w_uq': nrm((L, Q_RANK, H_A * (DN + DR)), Q_RANK ** -0.5),
        'kv_norm': gain((L, KV_RANK)),
        'w_uk': nrm((L, H_A, KV_RANK, DN), KV_RANK ** -0.5),
        'w_uv': nrm((L, H_A, KV_RANK, DV), KV_RANK ** -0.5),
        'shift_mu': jax.nn.sigmoid(nrm((L, D_SHIFT), 1.0)),
        'w0': -2.0 + nrm((L, D_B), 0.5),
        'w_up': nrm((L, W_RANK, D_B), 0.5 * W_RANK ** -0.5),
        'a0': nrm((L, D_B), 0.5),
        'a_up': nrm((L, A_RANK, D_B), A_RANK ** -0.5),
        'g_up': nrm((L, G_RANK, D_B), G_RANK ** -0.5),
        'k_k': 0.85 + nrm((L, D_B), 0.05),
        'k_a': 1.0 + nrm((L, D_B), 0.05),
        'r_k': nrm((L, H_B, N_B), 0.1),
        'gn_gain': gain((L, D_B)),
        'gn_bias': nrm((L, D_B), 0.02),
        'w_out': nrm((L, D_MIX, D), D_MIX ** -0.5),
        'xattn_norm': gain((L, D)),
        'mem_kv_norm': gain((L, D)),
        'w_mq': nrm((L, D, D), D ** -0.5),
        'w_mk': nrm((L, D, D), D ** -0.5),
        'w_mv': nrm((L, D, D), D ** -0.5),
        'w_mo': nrm((L, D, D), D ** -0.5),
        'ffn2_norm': gain((L, D)),
        'ffn2_w_gate': nrm((L, D, D_FF), D ** -0.5),
        'ffn2_w_up': nrm((L, D, D_FF), D ** -0.5),
        'ffn2_w_down': nrm((L, D_FF, D), D_FF ** -0.5),
        'final_norm': gain((D,)),
    }


def reference(x_prompt, x_sample, mem_prompt, cache_ckv, cache_krope, cache_mem_k, cache_mem_v,
              state_wkv, state_shift, ffn1_norm, ffn1_w_gate, ffn1_w_up, ffn1_w_down, mix_norm, w_in,
              q_norm, w_uq, kv_norm, w_uk, w_uv, shift_mu, w0, w_up, a0, a_up, g_up, k_k, k_a, r_k,
              gn_gain, gn_bias, w_out, xattn_norm, mem_kv_norm, w_mq, w_mk, w_mv, w_mo,
              ffn2_norm, ffn2_w_gate, ffn2_w_up, ffn2_w_down, final_norm):
    b_p = x_prompt.shape[0]
    t_s = x_sample.shape[1]
    pos_p = jnp.arange(x_prompt.shape[1], dtype=jnp.int32)
    pos_s = PAST_LEN + jnp.arange(t_s, dtype=jnp.int32)
    kpos_s = jnp.arange(PAST_LEN + t_s, dtype=jnp.int32)
    wkv_zero = jnp.zeros((b_p, H_B, N_B, N_B), x_prompt.dtype)
    shift_zero = jnp.zeros((b_p, 1, D_SHIFT), x_prompt.dtype)
    xp, xs = x_prompt, x_sample
    ckv_p, kr_p, mk_p, mv_p, wkv_p, sh_p = [], [], [], [], [], []
    ckv_s, kr_s, wkv_s, sh_s = [], [], [], []
    for l in range(DEPTH):
        lp = {
            'ffn1_norm': ffn1_norm[l], 'ffn1_w_gate': ffn1_w_gate[l], 'ffn1_w_up': ffn1_w_up[l],
            'ffn1_w_down': ffn1_w_down[l], 'mix_norm': mix_norm[l], 'w_in': w_in[l],
            'q_norm': q_norm[l], 'w_uq': w_uq[l], 'kv_norm': kv_norm[l], 'w_uk': w_uk[l],
            'w_uv': w_uv[l], 'shift_mu': shift_mu[l], 'w0': w0[l], 'w_up': w_up[l], 'a0': a0[l],
            'a_up': a_up[l], 'g_up': g_up[l], 'k_k': k_k[l], 'k_a': k_a[l], 'r_k': r_k[l],
            'gn_gain': gn_gain[l], 'gn_bias': gn_bias[l], 'w_out': w_out[l],
            'xattn_norm': xattn_norm[l], 'mem_kv_norm': mem_kv_norm[l], 'w_mq': w_mq[l],
            'w_mk': w_mk[l], 'w_mv': w_mv[l], 'w_mo': w_mo[l], 'ffn2_norm': ffn2_norm[l],
            'ffn2_w_gate': ffn2_w_gate[l], 'ffn2_w_up': ffn2_w_up[l], 'ffn2_w_down': ffn2_w_down[l],
        }
        mk, mv = mem_kv(mem_prompt, lp)
        xp, c1, k1, s1, h1 = layer(xp, pos_p, pos_p, None, None, wkv_zero, shift_zero, mk, mv, lp)
        ckv_p.append(c1); kr_p.append(k1); mk_p.append(mk); mv_p.append(mv)
        wkv_p.append(s1); sh_p.append(h1)
        xs, c2, k2, s2, h2 = layer(xs, pos_s, kpos_s, cache_ckv[l], cache_krope[l], state_wkv[l],
                                   state_shift[l], cache_mem_k[l], cache_mem_v[l], lp)
        ckv_s.append(c2); kr_s.append(k2); wkv_s.append(s2); sh_s.append(h2)
    y_prompt = rmsnorm(xp, final_norm)
    y_sample = rmsnorm(xs, final_norm)
    return (y_prompt, y_sample,
            jnp.stack(ckv_p), jnp.stack(kr_p), jnp.stack(mk_p), jnp.stack(mv_p),
            jnp.stack(wkv_p), jnp.stack(sh_p),
            jnp.stack(ckv_s), jnp.stack(kr_s), jnp.stack(wkv_s), jnp.stack(sh_s))
```

```python
import functools
import math

import jax
import jax.numpy as jnp
from jax import lax
from jax.experimental import pallas as pl
from jax.experimental.pallas import tpu as pltpu

F32 = jnp.float32
BF16 = jnp.bfloat16

D_MODEL = 1024
CHUNK = 64
H_A, DN, DR, DV = 8, 64, 32, 64
Q_RANK, KV_RANK = 256, 128
DQK = KV_RANK + DR
ROPE_THETA = 10000.0
MLA_SCALE = 1.0 / math.sqrt(DN + DR)
H_B, N_B = 8, 64
D_B = H_B * N_B
W_RANK, A_RANK, G_RANK = 64, 64, 128
D_SHIFT = 3 * D_B + W_RANK + A_RANK + G_RANK
N_MEM, MEM_HEADS = 256, 4
MEM_HD = D_MODEL // MEM_HEADS
D_FF = 2816
EPS = 1e-6
GN_EPS = 64e-5
NEG_INF = -1e30

PROJ_HEAD = 512
D_PROJ = PROJ_HEAD + D_SHIFT

TOKEN_TILE = 256
VMEM_LIMIT = 56 * 1024 * 1024


def _cparams(*sem):
    return pltpu.CompilerParams(dimension_semantics=sem, vmem_limit_bytes=VMEM_LIMIT)


def _const_spec(shape):
    nd = len(shape)
    return pl.BlockSpec(shape, lambda *_: (0,) * nd, pipeline_mode=pl.Buffered(1))


def _rms(x, g):
    return x * lax.rsqrt(jnp.mean(x * x, axis=-1, keepdims=True) + EPS) * g


def _dot(a, b):
    return jnp.dot(a, b, preferred_element_type=F32)


def _dot_nt(a, b):
    return lax.dot_general(a, b, (((1,), (1,)), ((), ())), preferred_element_type=F32)


def _dot_tn(a, b):
    return lax.dot_general(a, b, (((0,), (0,)), ((), ())), preferred_element_type=F32)


def _ffn_kernel(x_ref, g_ref, wg_ref, wu_ref, wd_ref, gf_ref, o_ref, *, final_norm):
    x = x_ref[...]
    h = _rms(x, g_ref[...]).astype(BF16)
    gate = _dot(h, wg_ref[...])
    up = _dot(h, wu_ref[...])
    act = (gate * jax.nn.sigmoid(gate) * up).astype(BF16)
    y = x + 0.5 * _dot(act, wd_ref[...])
    if final_norm:
        y = _rms(y, gf_ref[...])
    o_ref[...] = y


def ffn_half(x, g, wg, wu, wd, gf, *, final_norm):
    n, d = x.shape
    tm = min(TOKEN_TILE, n)
    return pl.pallas_call(
        functools.partial(_ffn_kernel, final_norm=final_norm),
        grid=(n // tm,),
        in_specs=[pl.BlockSpec((tm, d), lambda i: (i, 0)), _const_spec(g.shape), _const_spec(wg.shape),
                  _const_spec(wu.shape), _const_spec(wd.shape), _const_spec(gf.shape)],
        out_specs=pl.BlockSpec((tm, d), lambda i: (i, 0)),
        out_shape=jax.ShapeDtypeStruct((n, d), F32),
        compiler_params=_cparams("parallel"),
        name="ffn_half",
    )(x, g, wg, wu, wd, gf)


def _proj_kernel(x_ref, g_ref, win_ref, qn_ref, wuq_ref, wuk_ref, kvn_ref, cq_ref, sq_ref, ck_ref, sk_ref,
                 q_ref, ckv_ref, kr_ref, pb_ref):
    h = _rms(x_ref[...], g_ref[...]).astype(BF16)
    proj = _dot(h, win_ref[...])
    c_q = proj[:, :Q_RANK]
    c_kv = proj[:, Q_RANK:Q_RANK + KV_RANK]
    o = Q_RANK + KV_RANK
    k_r, k_rot = proj[:, o:o + DR], proj[:, o + DR:o + 2 * DR]
    ckv_ref[...] = _rms(c_kv, kvn_ref[...])
    kr_ref[...] = k_r * ck_ref[...] + k_rot * sk_ref[...]
    pb_ref[...] = proj[:, PROJ_HEAD:]
    qf = _dot(_rms(c_q, qn_ref[...]).astype(BF16), wuq_ref[...])
    nq = H_A * DN
    q_rope = qf[:, nq:nq + H_A * DR] * cq_ref[...] + qf[:, nq + H_A * DR:] * sq_ref[...]
    for hd in range(H_A):
        q_lat = _dot_nt(qf[:, hd * DN:(hd + 1) * DN].astype(BF16), wuk_ref[hd])
        q_ref[hd, :, :KV_RANK] = q_lat.astype(BF16)
        q_ref[hd, :, KV_RANK:] = q_rope[:, hd * DR:(hd + 1) * DR].astype(BF16)


def mixer_proj(x, g, w_in_ext, q_norm, w_uq_ext, w_uk, kv_norm, cos_q, sin_q, cos_k, sin_k):
    n, d = x.shape
    tm = min(TOKEN_TILE, n)
    n_tab = cos_q.shape[0] // tm
    tok = lambda w: pl.BlockSpec((tm, w), lambda i: (i, 0))
    tab = lambda w: pl.BlockSpec((tm, w), lambda i: (i % n_tab, 0))
    return pl.pallas_call(
        _proj_kernel,
        grid=(n // tm,),
        in_specs=[tok(d), _const_spec(g.shape), _const_spec(w_in_ext.shape), _const_spec(q_norm.shape),
                  _const_spec(w_uq_ext.shape), _const_spec(w_uk.shape), _const_spec(kv_norm.shape),
                  tab(H_A * DR), tab(H_A * DR), tab(DR), tab(DR)],
        out_specs=[pl.BlockSpec((H_A, tm, DQK), lambda i: (0, i, 0)), tok(KV_RANK), tok(DR), tok(D_SHIFT)],
        out_shape=[jax.ShapeDtypeStruct((H_A, n, DQK), BF16), jax.ShapeDtypeStruct((n, KV_RANK), F32),
                   jax.ShapeDtypeStruct((n, DR), F32), jax.ShapeDtypeStruct((n, D_SHIFT), F32)],
        compiler_params=_cparams("parallel"),
        name="mixer_proj",
    )(x, g, w_in_ext, q_norm, w_uq_ext, w_uk, kv_norm, cos_q, sin_q, cos_k, sin_k)


def _mla_kernel(*refs, tq, kb, n_past, t_new, has_past):
    if has_past:
        q_ref, ckv_ref, kr_ref, pckv_ref, pkr_ref, wuv_ref, o_ref, kcat = refs
    else:
        q_ref, ckv_ref, kr_ref, wuv_ref, o_ref, kcat = refs
    i = pl.program_id(1)
    t_all = n_past + t_new

    @pl.when(i == 0)
    def _():
        if kcat.shape[0] > t_all:
            kcat[t_all:, :] = jnp.zeros((kcat.shape[0] - t_all, DQK), BF16)
        if has_past:
            kcat[:n_past, :KV_RANK] = pckv_ref[0].astype(BF16)
            kcat[:n_past, KV_RANK:] = pkr_ref[0].astype(BF16)
        kcat[n_past:t_all, :KV_RANK] = ckv_ref[0].astype(BF16)
        kcat[n_past:t_all, KV_RANK:] = kr_ref[0].astype(BF16)

    rows = H_A * tq
    q = q_ref[...].reshape(rows, DQK)
    t_in = lax.broadcasted_iota(jnp.int32, (rows, 1), 0) & (tq - 1)
    q_pos = n_past + i * tq + t_in
    limit = (q_pos // CHUNK + 1) * CHUNK
    n_blocks = (n_past + (i + 1) * tq + kb - 1) // kb

    def body(j, carry):
        m, l, acc = carry
        kblk = kcat[pl.ds(pl.multiple_of(j * kb, kb), kb), :]
        s = _dot_nt(q, kblk) * MLA_SCALE
        k_pos = j * kb + lax.broadcasted_iota(jnp.int32, (1, kb), 1)
        s = jnp.where(k_pos < limit, s, NEG_INF)
        m_new = jnp.maximum(m, jnp.max(s, axis=-1, keepdims=True))
        alpha = jnp.exp(m - m_new)
        p = jnp.exp(s - m_new)
        l = alpha * l + jnp.sum(p, axis=-1, keepdims=True)
        acc = alpha * acc + _dot(p.astype(BF16), kblk[:, :KV_RANK])
        return m_new, l, acc

    m0 = jnp.full((rows, 1), NEG_INF, F32)
    l0 = jnp.zeros((rows, 1), F32)
    a0 = jnp.zeros((rows, KV_RANK), F32)
    _, l, acc = lax.fori_loop(0, n_blocks, body, (m0, l0, a0))
    o_lat = (acc / l).astype(BF16)
    outs = [_dot(o_lat[hd * tq:(hd + 1) * tq], wuv_ref[hd]) for hd in range(H_A)]
    o_ref[0] = jnp.concatenate(outs, axis=-1).astype(o_ref.dtype)


def mla_attention(q, ckv, kr, past_ckv, past_kr, w_uv, *, tq, kb):
    b, t, _ = ckv.shape
    has_past = past_ckv is not None
    n_past = past_ckv.shape[1] if has_past else 0
    t_pad = -(-(n_past + t) // kb) * kb
    nq = t // tq
    stream = lambda n, w: pl.BlockSpec((1, n, w), lambda bi, i: (bi, 0, 0))
    in_specs = [pl.BlockSpec((H_A, tq, DQK), lambda bi, i: (0, bi * nq + i, 0)), stream(t, KV_RANK), stream(t, DR)]
    args = [q, ckv, kr]
    if has_past:
        in_specs += [stream(n_past, KV_RANK), stream(n_past, DR)]
        args += [past_ckv, past_kr]
    in_specs.append(_const_spec(w_uv.shape))
    args.append(w_uv)
    return pl.pallas_call(
        functools.partial(_mla_kernel, tq=tq, kb=kb, n_past=n_past, t_new=t, has_past=has_past),
        grid=(b, nq),
        in_specs=in_specs,
        out_specs=pl.BlockSpec((1, tq, H_A * DV), lambda bi, i: (bi, i, 0)),
        out_shape=jax.ShapeDtypeStruct((b, t, H_A * DV), BF16),
        scratch_shapes=[pltpu.VMEM((t_pad, DQK), BF16)],
        compiler_params=_cparams("parallel", "arbitrary"),
        name="mla_attention",
    )(*args)


def _split3(x):
    hi = x.astype(BF16)
    r1 = x - hi.astype(F32)
    mid = r1.astype(BF16)
    lo = (r1 - mid.astype(F32)).astype(BF16)
    return hi, mid, lo


def _wkv_kernel(pb_ref, sh0_ref, s0_ref, mu_ref, w0_ref, wup_ref, a0_ref, aup_ref, gup_ref, kk_ref, ka_ref,
                rk_ref, gng_ref, gnb_ref, y_ref, so_ref, s_scr, prev_scr):
    c = pl.program_id(1)

    @pl.when(c == 0)
    def _():
        s_scr[...] = s0_ref[0]
        prev_scr[...] = sh0_ref[0]

    pb = pb_ref[0]
    row = lax.broadcasted_iota(jnp.int32, (CHUNK, 1), 0)
    col = lax.broadcasted_iota(jnp.int32, (1, CHUNK), 1)
    prev = jnp.where(row == 0, prev_scr[...], pltpu.roll(pb, shift=1, axis=0))
    prev_scr[...] = pb[CHUNK - 1:, :]
    xs = pb + mu_ref[...] * (prev - pb)
    r = xs[:, :D_B]
    k = xs[:, D_B:2 * D_B]
    v = xs[:, 2 * D_B:3 * D_B]
    o = 3 * D_B
    xw, xa, xg = xs[:, o:o + W_RANK], xs[:, o + W_RANK:o + W_RANK + A_RANK], xs[:, o + W_RANK + A_RANK:]
    z = w0_ref[...] + _dot(jnp.tanh(xw).astype(BF16), wup_ref[...])
    logw = -math.exp(-0.5) * jax.nn.sigmoid(z)
    gate = jax.nn.sigmoid(a0_ref[...] + _dot(xa.astype(BF16), aup_ref[...]))
    g = _dot(jax.nn.sigmoid(xg).astype(BF16), gup_ref[...])
    kkf = k * kk_ref[...]
    k2 = k * (1.0 + (gate - 1.0) * ka_ref[...])
    tri_incl = (row >= col)
    ones_tri = jnp.where(tri_incl, 1.0, 0.0).astype(BF16)
    cs = sum(_dot(ones_tri, piece) for piece in _split3(logw))
    p_in = jnp.exp(cs)
    p_prev = jnp.exp(cs - logw)
    p_inv = jnp.exp(-cs)
    p_end = p_in[CHUNK - 1:, :]
    tri_strict = row > col
    diag = row == col

    outs = []
    for hd in range(H_B):
        sl = slice(hd * N_B, (hd + 1) * N_B)
        kk_h = kkf[:, sl]
        kk_h = kk_h * lax.rsqrt(jnp.sum(kk_h * kk_h, axis=-1, keepdims=True) + 1e-12)
        r_h, k_h, v_h, gate_h = r[:, sl], k2[:, sl], v[:, sl], gate[:, sl]
        at = (-kk_h * p_prev[:, sl]).astype(BF16)
        kt32 = k_h * p_inv[:, sl]
        bt32 = kk_h * gate_h * p_inv[:, sl]
        kt, bt = kt32.astype(BF16), bt32.astype(BF16)
        rt32 = r_h * p_in[:, sl]
        rt = rt32.astype(BF16)
        kh = (kt32 * p_end[:, sl]).astype(BF16)
        bh = (bt32 * p_end[:, sl]).astype(BF16)
        vb = v_h.astype(BF16)
        a_ab = jnp.where(tri_strict, _dot_nt(at, bt), 0.0)
        a_ak = jnp.where(tri_strict, _dot_nt(at, kt), 0.0)
        a_rk = jnp.where(tri_incl, _dot_nt(rt, kt), 0.0)
        a_rb = jnp.where(tri_incl, _dot_nt(rt, bt), 0.0).astype(BF16)
        zz = jnp.concatenate([at.astype(F32), _dot(a_ak.astype(BF16), vb)], axis=-1)
        x = a_ab
        for lvl in range(6):
            xb = x.astype(BF16)
            zz = zz + _dot(xb, zz.astype(BF16))
            if lvl < 5:
                x = _dot(xb, xb)
        w_m, u0 = zz[:, :N_B].astype(BF16), zz[:, N_B:].astype(BF16)
        s_old = s_scr[hd]
        s_b = s_old.astype(BF16)
        m_mat = jnp.where(diag, p_end[:, sl], 0.0) + _dot_tn(w_m, bh)
        g_mat = _dot_tn(vb, kh) + _dot_tn(u0, bh)
        q_hat = (rt32 + _dot(a_rb, w_m)).astype(BF16)
        y = _dot_nt(q_hat, s_b) + _dot(a_rk.astype(BF16), vb) + _dot(a_rb, u0)
        s_scr[hd] = _dot(s_b, m_mat.astype(BF16)) + g_mat
        mean = jnp.mean(y, axis=-1, keepdims=True)
        var = jnp.mean(jnp.square(y - mean), axis=-1, keepdims=True)
        y = (y - mean) * lax.rsqrt(var + GN_EPS) * gng_ref[:, sl] + gnb_ref[:, sl]
        y = y + jnp.sum(r_h * k_h * rk_ref[:, sl], axis=-1, keepdims=True) * v_h
        outs.append(y)
    y_ref[0] = (jnp.concatenate(outs, axis=-1) * g).astype(y_ref.dtype)

    @pl.when(c == pl.num_programs(1) - 1)
    def _():
        so_ref[0] = s_scr[...]


def wkv_group(pb, shift0, s0, mu, w0, w_up, a0, a_up, g_up, k_k, k_a, r_k, gn_g, gn_b):
    b, t, _ = pb.shape
    small = [mu, w0, w_up, a0, a_up, g_up, k_k, k_a, r_k, gn_g, gn_b]
    return pl.pallas_call(
        _wkv_kernel,
        grid=(b, t // CHUNK),
        in_specs=[pl.BlockSpec((1, CHUNK, D_SHIFT), lambda bi, c: (bi, c, 0)),
                  pl.BlockSpec((1, 1, D_SHIFT), lambda bi, c: (bi, 0, 0)),
                  pl.BlockSpec((1, H_B, N_B, N_B), lambda bi, c: (bi, 0, 0, 0))]
        + [_const_spec(a.shape) for a in small],
        out_specs=[pl.BlockSpec((1, CHUNK, D_B), lambda bi, c: (bi, c, 0)),
                   pl.BlockSpec((1, H_B, N_B, N_B), lambda bi, c: (bi, 0, 0, 0))],
        out_shape=[jax.ShapeDtypeStruct((b, t, D_B), BF16), jax.ShapeDtypeStruct((b, H_B, N_B, N_B), F32)],
        scratch_shapes=[pltpu.VMEM((H_B, N_B, N_B), F32), pltpu.VMEM((1, D_SHIFT), F32)],
        compiler_params=_cparams("parallel", "arbitrary"),
        name="wkv_group",
    )(pb, shift0, s0, *small)


def _out_kernel(x_ref, ya_ref, yb_ref, wo_ref, g_ref, wq_ref, mk_ref, mv_ref, wmo_ref, o_ref, *, spt):
    x = x_ref[...] + _dot(ya_ref[...], wo_ref[:H_A * DV, :]) + _dot(yb_ref[...], wo_ref[H_A * DV:, :])
    q = _dot(_rms(x, g_ref[...]).astype(BF16), wq_ref[...])
    ts = x.shape[0] // spt
    rows = []
    for s in range(spt):
        heads = []
        for hd in range(MEM_HEADS):
            sl = slice(hd * MEM_HD, (hd + 1) * MEM_HD)
            sc = _dot_nt(q[s * ts:(s + 1) * ts, sl].astype(BF16), mk_ref[s, :, sl].astype(BF16)) / math.sqrt(MEM_HD)
            sc = jnp.exp(sc - jnp.max(sc, axis=-1, keepdims=True))
            p = sc / jnp.sum(sc, axis=-1, keepdims=True)
            heads.append(_dot(p.astype(BF16), mv_ref[s, :, sl].astype(BF16)))
        rows.append(jnp.concatenate(heads, axis=-1))
    att = rows[0] if spt == 1 else jnp.concatenate(rows, axis=0)
    o_ref[...] = x + _dot(att.astype(BF16), wmo_ref[...])


def mixer_out_xattn(x, ya, yb, w_out, g, w_mq, mk, mv, w_mo, *, t_stream):
    n, d = x.shape
    tm = min(TOKEN_TILE, n)
    spt = max(1, tm // t_stream)
    tps = max(1, t_stream // tm)
    tok = lambda w: pl.BlockSpec((tm, w), lambda i: (i, 0))
    mem = pl.BlockSpec((spt, N_MEM, d), lambda i: (i // tps, 0, 0))
    return pl.pallas_call(
        functools.partial(_out_kernel, spt=spt),
        grid=(n // tm,),
        in_specs=[tok(d), tok(H_A * DV), tok(D_B), _const_spec(w_out.shape), _const_spec(g.shape),
                  _const_spec(w_mq.shape), mem, mem, _const_spec(w_mo.shape)],
        out_specs=tok(d),
        out_shape=jax.ShapeDtypeStruct((n, d), F32),
        compiler_params=_cparams("parallel"),
        name="mixer_out_xattn",
    )(x, ya, yb, w_out, g, w_mq, mk, mv, w_mo)


def _memkv_kernel(m_ref, g_ref, wk_ref, wv_ref, k_ref, v_ref):
    m = _rms(m_ref[...], g_ref[...]).astype(BF16)
    k_ref[...] = _dot(m, wk_ref[...])
    v_ref[...] = _dot(m, wv_ref[...])


def mem_kv(mem, g, w_mk, w_mv):
    n, d = mem.shape
    tm = min(TOKEN_TILE, n)
    tok = pl.BlockSpec((tm, d), lambda i: (i, 0))
    return pl.pallas_call(
        _memkv_kernel,
        grid=(n // tm,),
        in_specs=[tok, _const_spec(g.shape), _const_spec(w_mk.shape), _const_spec(w_mv.shape)],
        out_specs=[tok, tok],
        out_shape=[jax.ShapeDtypeStruct((n, d), F32)] * 2,
        compiler_params=_cparams("parallel"),
        name="mem_kv",
    )(mem, g, w_mk, w_mv)


def _rot_half_cols(w):
    half = DR // 2
    shp = w.shape
    w = w.reshape(shp[:-1] + (shp[-1] // DR, 2, half))
    return jnp.concatenate([-w[..., 1:, :], w[..., :1, :]], axis=-2).reshape(shp)


def _rope_tables(pos, reps):
    half = DR // 2
    inv = ROPE_THETA ** (-jnp.arange(half, dtype=F32) / half)
    ang = pos.astype(F32)[:, None] * inv[None, :]
    cos = jnp.tile(jnp.cos(ang), (1, 2 * reps))
    sin = jnp.tile(jnp.sin(ang), (1, 2 * reps))
    return cos, sin


def _group_tables(pos, n_tokens):
    rows = max(pos.shape[0], min(TOKEN_TILE, n_tokens))
    pos = jnp.tile(pos, rows // pos.shape[0])
    return _rope_tables(pos, H_A) + _rope_tables(pos, 1)


def _layer_group(x, tabs, lw, l, b, t, past_ckv, past_kr, wkv0, shift0, mk, mv, *, tq, kb, final_norm):
    row = lambda a: a[l][None, :]
    x = ffn_half(x, row(lw["ffn1_norm"]), lw["ffn1_wg"][l], lw["ffn1_wu"][l], lw["ffn1_wd"][l],
                 lw["final_norm"], final_norm=False)
    q, ckv, kr, pb = mixer_proj(x, row(lw["mix_norm"]), lw["w_in_ext"][l], row(lw["q_norm"]), lw["w_uq_ext"][l],
                                lw["w_uk"][l], row(lw["kv_norm"]), *tabs)
    ckv3, kr3 = ckv.reshape(b, t, KV_RANK), kr.reshape(b, t, DR)
    ya = mla_attention(q, ckv3, kr3, past_ckv, past_kr, lw["w_uv"][l], tq=tq, kb=kb)
    pb3 = pb.reshape(b, t, D_SHIFT)
    yb, wkv = wkv_group(pb3, shift0, wkv0, row(lw["shift_mu"]), row(lw["w0"]), lw["w_up"][l], row(lw["a0"]),
                        lw["a_up"][l], lw["g_up"][l], row(lw["k_k"]), row(lw["k_a"]), row(lw["r_k"]),
                        row(lw["gn_gain"]), row(lw["gn_bias"]))
    x = mixer_out_xattn(x, ya.reshape(b * t, H_A * DV), yb.reshape(b * t, D_B), lw["w_out"][l],
                        row(lw["xattn_norm"]), lw["w_mq"][l], mk, mv, lw["w_mo"][l], t_stream=t)
    x = ffn_half(x, row(lw["ffn2_norm"]), lw["ffn2_wg"][l], lw["ffn2_wu"][l], lw["ffn2_wd"][l],
                 lw["final_norm"], final_norm=final_norm)
    return x, ckv3, kr3, wkv, pb3[:, -1:, :]


def kernel(x_prompt, x_sample, mem_prompt, cache_ckv, cache_krope, cache_mem_k, cache_mem_v, state_wkv, state_shift, ffn1_norm, ffn1_w_gate, ffn1_w_up, ffn1_w_down, mix_norm, w_in, q_norm, w_uq, kv_norm, w_uk, w_uv, shift_mu, w0, w_up, a0, a_up, g_up, k_k, k_a, r_k, gn_gain, gn_bias, w_out, xattn_norm, mem_kv_norm, w_mq, w_mk, w_mv, w_mo, ffn2_norm, ffn2_w_gate, ffn2_w_up, ffn2_w_down, final_norm):
    depth = w_in.shape[0]
    b_p, t_p, d = x_prompt.shape
    b_s, t_s, _ = x_sample.shape
    n_past = cache_ckv.shape[2]
    bf = lambda a: a.astype(BF16)

    o = Q_RANK + KV_RANK
    w_kr = w_in[..., o:o + DR]
    w_in_ext = jnp.concatenate(
        [w_in[..., :o + DR], _rot_half_cols(w_kr), jnp.zeros(w_in.shape[:2] + (PROJ_HEAD - o - 2 * DR,), w_in.dtype),
         w_in[..., o + DR:]], axis=-1)
    uq = w_uq.reshape(depth, Q_RANK, H_A, DN + DR)
    uq_nope = uq[..., :DN].reshape(depth, Q_RANK, H_A * DN)
    uq_rope = uq[..., DN:].reshape(depth, Q_RANK, H_A * DR)
    w_uq_ext = jnp.concatenate([uq_nope, uq_rope, _rot_half_cols(uq_rope)], axis=-1)
    lw = dict(
        ffn1_norm=ffn1_norm, ffn1_wg=bf(ffn1_w_gate), ffn1_wu=bf(ffn1_w_up), ffn1_wd=bf(ffn1_w_down),
        mix_norm=mix_norm, w_in_ext=bf(w_in_ext), q_norm=q_norm, w_uq_ext=bf(w_uq_ext), kv_norm=kv_norm,
        w_uk=bf(w_uk), w_uv=bf(w_uv), shift_mu=shift_mu, w0=w0, w_up=bf(w_up), a0=a0, a_up=bf(a_up), g_up=bf(g_up),
        k_k=k_k, k_a=k_a, r_k=r_k.reshape(depth, D_B), gn_gain=gn_gain, gn_bias=gn_bias, w_out=bf(w_out),
        xattn_norm=xattn_norm, w_mq=bf(w_mq), w_mo=bf(w_mo), ffn2_norm=ffn2_norm, ffn2_wg=bf(ffn2_w_gate),
        ffn2_wu=bf(ffn2_w_up), ffn2_wd=bf(ffn2_w_down), final_norm=final_norm[None, :])
    w_mk_b, w_mv_b = bf(w_mk), bf(w_mv)

    tabs_p = _group_tables(jnp.arange(t_p, dtype=jnp.int32), b_p * t_p)
    tabs_s = _group_tables(n_past + jnp.arange(t_s, dtype=jnp.int32), b_s * t_s)
    wkv_zero = jnp.zeros((b_p, H_B, N_B, N_B), x_prompt.dtype)
    shift_zero = jnp.zeros((b_p, 1, D_SHIFT), x_prompt.dtype)

    xp = x_prompt.reshape(b_p * t_p, d)
    xs = x_sample.reshape(b_s * t_s, d)
    mem2 = mem_prompt.reshape(b_p * N_MEM, d)
    outs_p = [[] for _ in range(6)]
    outs_s = [[] for _ in range(4)]
    for l in range(depth):
        last = l == depth - 1
        mk, mv = mem_kv(mem2, mem_kv_norm[l][None, :], w_mk_b[l], w_mv_b[l])
        mk, mv = mk.reshape(b_p, N_MEM, d), mv.reshape(b_p, N_MEM, d)
        xp, c1, k1, s1, h1 = _layer_group(xp, tabs_p, lw, l, b_p, t_p, None, None, wkv_zero, shift_zero, mk, mv,
                                          tq=128, kb=512, final_norm=last)
        for acc, val in zip(outs_p, (c1, k1, mk.reshape(b_p, N_MEM, MEM_HEADS, MEM_HD),
                                     mv.reshape(b_p, N_MEM, MEM_HEADS, MEM_HD), s1, h1)):
            acc.append(val)
        kb_s = -(-(n_past + t_s) // 128) * 128
        xs, c2, k2, s2, h2 = _layer_group(xs, tabs_s, lw, l, b_s, t_s, cache_ckv[l], cache_krope[l], state_wkv[l],
                                          state_shift[l], cache_mem_k[l].reshape(b_s, N_MEM, d),
                                          cache_mem_v[l].reshape(b_s, N_MEM, d), tq=t_s, kb=kb_s, final_norm=last)
        for acc, val in zip(outs_s, (c2, k2, s2, h2)):
            acc.append(val)
    stack = lambda seq: jnp.stack(seq)
    return (xp.reshape(b_p, t_p, d), xs.reshape(b_s, t_s, d),
            stack(outs_p[0]), stack(outs_p[1]), stack(outs_p[2]), stack(outs_p[3]), stack(outs_p[4]), stack(outs_p[5]),
            stack(outs_s[0]), stack(outs_s[1]), stack(outs_s[2]), stack(outs_s[3]))
```

```python
import functools
import math

import jax
import jax.numpy as jnp
from jax import lax
from jax.experimental import pallas as pl
from jax.experimental.pallas import tpu as pltpu

F32 = jnp.float32
BF16 = jnp.bfloat16

D_MODEL = 1024
CHUNK = 64
H_A, DN, DR, DV = 8, 64, 32, 64
Q_RANK, KV_RANK = 256, 128
DQK = KV_RANK + DR
ROPE_THETA = 10000.0
MLA_SCALE = 1.0 / math.sqrt(DN + DR)
LOG2E = math.log2(math.e)
H_B, N_B = 8, 64
D_B = H_B * N_B
W_RANK, A_RANK, G_RANK = 64, 64, 128
D_SHIFT = 3 * D_B + W_RANK + A_RANK + G_RANK
N_MEM, MEM_HEADS = 256, 4
MEM_HD = D_MODEL // MEM_HEADS
D_FF = 2816
EPS = 1e-6
GN_EPS = 64e-5
NEG_INF = -1e30

PROJ_HEAD = 512
D_PROJ = PROJ_HEAD + D_SHIFT

TOKEN_TILE = 256
WKV_STREAMS = 4
VMEM_LIMIT = 56 * 1024 * 1024


def _cparams(*sem):
    return pltpu.CompilerParams(dimension_semantics=sem, vmem_limit_bytes=VMEM_LIMIT)


def _layer_spec(arr, l):
    nd = arr.ndim - 1
    return pl.BlockSpec((None,) + arr.shape[1:], lambda *_: (l,) + (0,) * nd, pipeline_mode=pl.Buffered(1))


def _rms(x, g):
    return x * lax.rsqrt(jnp.mean(x * x, axis=-1, keepdims=True) + EPS) * g


def _dot(a, b):
    return jnp.dot(a, b, preferred_element_type=F32)


def _dot_nt(a, b):
    return lax.dot_general(a, b, (((1,), (1,)), ((), ())), preferred_element_type=F32)


def _bmm(a, b):
    return jnp.einsum("gmk,gkn->gmn", a, b, preferred_element_type=F32)


def _bmm_nt(a, b):
    return jnp.einsum("gmk,gnk->gmn", a, b, preferred_element_type=F32)


def _ffn_kernel(x_ref, g_ref, wg_ref, wu_ref, wd_ref, gf_ref, o_ref, *, final_norm):
    x = x_ref[...]
    h = _rms(x, g_ref[...]).astype(BF16)
    gate = _dot(h, wg_ref[...])
    up = _dot(h, wu_ref[...])
    act = (gate * jax.nn.sigmoid(gate) * up).astype(BF16)
    y = x + 0.5 * _dot(act, wd_ref[...])
    if final_norm:
        y = _rms(y, gf_ref[...])
    o_ref[...] = y


def ffn_half(x, l, g, wg, wu, wd, gf, *, final_norm):
    n, d = x.shape
    tm = min(TOKEN_TILE, n)
    return pl.pallas_call(
        functools.partial(_ffn_kernel, final_norm=final_norm),
        grid=(n // tm,),
        in_specs=[pl.BlockSpec((tm, d), lambda i: (i, 0)), _layer_spec(g, l), _layer_spec(wg, l),
                  _layer_spec(wu, l), _layer_spec(wd, l), _layer_spec(gf, 0)],
        out_specs=pl.BlockSpec((tm, d), lambda i: (i, 0)),
        out_shape=jax.ShapeDtypeStruct((n, d), F32),
        compiler_params=_cparams("parallel"),
        name="ffn_half",
    )(x, g, wg, wu, wd, gf)


def _proj_kernel(x_ref, g_ref, win_ref, qn_ref, wuq_ref, wuk_ref, kvn_ref, cq_ref, sq_ref, ck_ref, sk_ref,
                 q_ref, ckv_ref, kr_ref, pb_ref):
    h = _rms(x_ref[...], g_ref[...]).astype(BF16)
    proj = _dot(h, win_ref[...])
    c_q = proj[:, :Q_RANK]
    c_kv = proj[:, Q_RANK:Q_RANK + KV_RANK]
    o = Q_RANK + KV_RANK
    k_r, k_rot = proj[:, o:o + DR], proj[:, o + DR:o + 2 * DR]
    ckv_ref[...] = _rms(c_kv, kvn_ref[...])
    kr_ref[...] = k_r * ck_ref[...] + k_rot * sk_ref[...]
    pb_ref[...] = proj[:, PROJ_HEAD:]
    qf = _dot(_rms(c_q, qn_ref[...]).astype(BF16), wuq_ref[...])
    nq = H_A * DN
    q_rope = (qf[:, nq:nq + H_A * DR] * cq_ref[...] + qf[:, nq + H_A * DR:] * sq_ref[...]) * (MLA_SCALE * LOG2E)
    for hd in range(H_A):
        q_lat = _dot_nt(qf[:, hd * DN:(hd + 1) * DN].astype(BF16), wuk_ref[hd]) * (MLA_SCALE * LOG2E)
        q_ref[hd, :, :KV_RANK] = q_lat.astype(BF16)
        q_ref[hd, :, KV_RANK:] = q_rope[:, hd * DR:(hd + 1) * DR].astype(BF16)


def mixer_proj(x, l, g, w_in_ext, q_norm, w_uq_ext, w_uk, kv_norm, cos_q, sin_q, cos_k, sin_k):
    n, d = x.shape
    tm = min(TOKEN_TILE, n)
    n_tab = cos_q.shape[0] // tm
    tok = lambda w: pl.BlockSpec((tm, w), lambda i: (i, 0))
    tab = lambda w: pl.BlockSpec((tm, w), lambda i: (i % n_tab, 0))
    return pl.pallas_call(
        _proj_kernel,
        grid=(n // tm,),
        in_specs=[tok(d), _layer_spec(g, l), _layer_spec(w_in_ext, l), _layer_spec(q_norm, l),
                  _layer_spec(w_uq_ext, l), _layer_spec(w_uk, l), _layer_spec(kv_norm, l),
                  tab(H_A * DR), tab(H_A * DR), tab(DR), tab(DR)],
        out_specs=[pl.BlockSpec((H_A, tm, DQK), lambda i: (0, i, 0)), tok(KV_RANK), tok(DR), tok(D_SHIFT)],
        out_shape=[jax.ShapeDtypeStruct((H_A, n, DQK), BF16), jax.ShapeDtypeStruct((n, KV_RANK), F32),
                   jax.ShapeDtypeStruct((n, DR), F32), jax.ShapeDtypeStruct((n, D_SHIFT), F32)],
        compiler_params=_cparams("parallel"),
        name="mixer_proj",
    )(x, g, w_in_ext, q_norm, w_uq_ext, w_uk, kv_norm, cos_q, sin_q, cos_k, sin_k)


def _mla_kernel(*refs, tq, kb, n_past, t_new, has_past):
    if has_past:
        q_ref, ckv_ref, kr_ref, pckv_ref, pkr_ref, wuv_ref, o_ref, kcat = refs
    else:
        q_ref, ckv_ref, kr_ref, wuv_ref, o_ref, kcat = refs
    i = pl.program_id(1)
    t_all = n_past + t_new

    @pl.when(i == 0)
    def _():
        if kcat.shape[0] > t_all:
            kcat[t_all:, :] = jnp.zeros((kcat.shape[0] - t_all, DQK), BF16)
        if has_past:
            kcat[:n_past, :KV_RANK] = pckv_ref[0].astype(BF16)
            kcat[:n_past, KV_RANK:] = pkr_ref[0].astype(BF16)
        kcat[n_past:t_all, :KV_RANK] = ckv_ref[0].astype(BF16)
        kcat[n_past:t_all, KV_RANK:] = kr_ref[0].astype(BF16)

    rows = H_A * tq
    q = q_ref[...].reshape(rows, DQK)
    t_in = lax.broadcasted_iota(jnp.int32, (rows, 1), 0) & (tq - 1)
    q_pos = n_past + i * tq + t_in
    limit = (q_pos // CHUNK + 1) * CHUNK
    first_limit = ((n_past + i * tq) // CHUNK + 1) * CHUNK
    n_full = first_limit // kb
    n_blocks = (n_past + (i + 1) * tq + kb - 1) // kb

    def step(j, carry, masked):
        m, l, acc = carry
        kblk = kcat[pl.ds(pl.multiple_of(j * kb, kb), kb), :]
        s = _dot_nt(q, kblk)
        if masked:
            k_pos = j * kb + lax.broadcasted_iota(jnp.int32, (1, kb), 1)
            s = jnp.where(k_pos < limit, s, NEG_INF)
        m_new = jnp.maximum(m, jnp.max(s, axis=-1, keepdims=True))
        alpha = jnp.exp2(m - m_new)
        p = jnp.exp2(s - m_new)
        l = alpha * l + jnp.sum(p, axis=-1, keepdims=True)
        acc = alpha * acc + _dot(p.astype(BF16), kblk[:, :KV_RANK])
        return m_new, l, acc

    m0 = jnp.full((rows, 1), NEG_INF, F32)
    l0 = jnp.zeros((rows, 1), F32)
    a0 = jnp.zeros((rows, KV_RANK), F32)
    carry = lax.fori_loop(0, n_full, functools.partial(step, masked=False), (m0, l0, a0))
    _, l, acc = lax.fori_loop(n_full, n_blocks, functools.partial(step, masked=True), carry)
    o_lat = (acc / l).astype(BF16)
    outs = [_dot(o_lat[hd * tq:(hd + 1) * tq], wuv_ref[hd]) for hd in range(H_A)]
    o_ref[0] = jnp.concatenate(outs, axis=-1).astype(o_ref.dtype)


def mla_attention(q, ckv, kr, l, past_ckv, past_kr, w_uv, *, tq, kb):
    b, t, _ = ckv.shape
    has_past = past_ckv is not None
    n_past = past_ckv.shape[2] if has_past else 0
    t_pad = -(-(n_past + t) // kb) * kb
    nq = t // tq
    stream = lambda n, w: pl.BlockSpec((1, n, w), lambda bi, i: (bi, 0, 0))
    past = lambda w: pl.BlockSpec((None, 1, n_past, w), lambda bi, i: (l, bi, 0, 0))
    in_specs = [pl.BlockSpec((H_A, tq, DQK), lambda bi, i: (0, bi * nq + i, 0)), stream(t, KV_RANK), stream(t, DR)]
    args = [q, ckv, kr]
    if has_past:
        in_specs += [past(KV_RANK), past(DR)]
        args += [past_ckv, past_kr]
    in_specs.append(_layer_spec(w_uv, l))
    args.append(w_uv)
    return pl.pallas_call(
        functools.partial(_mla_kernel, tq=tq, kb=kb, n_past=n_past, t_new=t, has_past=has_past),
        grid=(b, nq),
        in_specs=in_specs,
        out_specs=pl.BlockSpec((1, tq, H_A * DV), lambda bi, i: (bi, i, 0)),
        out_shape=jax.ShapeDtypeStruct((b, t, H_A * DV), BF16),
        scratch_shapes=[pltpu.VMEM((t_pad, DQK), BF16)],
        compiler_params=_cparams("parallel", "arbitrary"),
        name="mla_attention",
    )(*args)


def _split3(x):
    hi = x.astype(BF16)
    r1 = x - hi.astype(F32)
    mid = r1.astype(BF16)
    lo = (r1 - mid.astype(F32)).astype(BF16)
    return hi, mid, lo


def _wkv_kernel(pb_ref, sh0_ref, s0_ref, mu_ref, w0_ref, wup_ref, a0_ref, aup_ref, gup_ref, kk_ref, ka_ref,
                rk_ref, gng_ref, gnb_ref, y_ref, so_ref, st_scr, prev_scr, *, sb):
    c = pl.program_id(1)
    nb = sb * H_B
    n = sb * CHUNK

    @pl.when(c == 0)
    def _():
        st_scr[...] = jnp.swapaxes(s0_ref[...].reshape(nb, N_B, N_B), 1, 2)
        prev_scr[...] = sh0_ref[...]

    row = lax.broadcasted_iota(jnp.int32, (CHUNK, 1), 0)
    parts = []
    for s in range(sb):
        pb = pb_ref[s]
        prev = jnp.where(row == 0, prev_scr[s], pltpu.roll(pb, shift=1, axis=0))
        prev_scr[s] = pb[CHUNK - 1:, :]
        parts.append(pb + mu_ref[...] * (prev - pb))
    xs = jnp.concatenate(parts, axis=0)
    r = xs[:, :D_B]
    k = xs[:, D_B:2 * D_B]
    v = xs[:, 2 * D_B:3 * D_B]
    o = 3 * D_B
    xw, xa, xg = xs[:, o:o + W_RANK], xs[:, o + W_RANK:o + W_RANK + A_RANK], xs[:, o + W_RANK + A_RANK:]
    z = w0_ref[...] + _dot(jnp.tanh(xw).astype(BF16), wup_ref[...])
    logw = -math.exp(-0.5) * jax.nn.sigmoid(z)
    gate = jax.nn.sigmoid(a0_ref[...] + _dot(xa.astype(BF16), aup_ref[...]))
    g = _dot(jax.nn.sigmoid(xg).astype(BF16), gup_ref[...])
    kkf = k * kk_ref[...]
    k2 = k * (1.0 + (gate - 1.0) * ka_ref[...])
    ri = lax.broadcasted_iota(jnp.int32, (n, 1), 0)
    ci = lax.broadcasted_iota(jnp.int32, (1, n), 1)
    same_stream = jnp.right_shift(ri, 6) == jnp.right_shift(ci, 6)
    ones_tri = jnp.where(ri >= ci, jnp.where(same_stream, 1.0, 0.0), 0.0).astype(BF16)
    cs = sum(_dot(ones_tri, piece) for piece in _split3(logw))
    p_in = jnp.exp(cs)
    p_prev = jnp.exp(cs - logw)
    p_inv = jnp.exp(-cs)

    def heads3(x):
        return jnp.stack([x[s * CHUNK:(s + 1) * CHUNK, h * N_B:(h + 1) * N_B]
                          for s in range(sb) for h in range(H_B)], axis=0)

    def per_head(ref):
        return jnp.stack([ref[:, h * N_B:(h + 1) * N_B] for _ in range(sb) for h in range(H_B)], axis=0)

    r3, k3, v3 = heads3(r), heads3(k2), heads3(v)
    kkf3, kg3 = heads3(kkf), heads3(kkf * gate)
    pin3, pprev3, pinv3 = heads3(p_in), heads3(p_prev), heads3(p_inv)
    rs = lax.rsqrt(jnp.sum(kkf3 * kkf3, axis=-1, keepdims=True) + 1e-12)
    pend3 = pin3[:, CHUNK - 1:, :]
    at32 = -(kkf3 * rs) * pprev3
    kt32 = k3 * pinv3
    bt32 = kg3 * rs * pinv3
    rt32 = r3 * pin3
    at, rt = at32.astype(BF16), rt32.astype(BF16)
    bk = jnp.concatenate([bt32, kt32], axis=1).astype(BF16)
    bhkh_t = jnp.swapaxes(jnp.concatenate([bt32 * pend3, kt32 * pend3], axis=1), 1, 2).astype(BF16)
    vb = v3.astype(BF16)

    tt = lax.broadcasted_iota(jnp.int32, (1, CHUNK, 2 * CHUNK), 1)
    ss = lax.broadcasted_iota(jnp.int32, (1, CHUNK, 2 * CHUNK), 2) & (CHUNK - 1)
    a_top = jnp.where(tt > ss, _bmm_nt(at, bk), 0.0)
    a_bot = jnp.where(tt >= ss, _bmm_nt(rt, bk), 0.0).astype(BF16)
    zero_v = jnp.concatenate([jnp.zeros((nb, CHUNK, N_B), BF16), vb], axis=1)
    zz = jnp.concatenate([at32, _bmm(a_top.astype(BF16), zero_v)], axis=-1)
    x = a_top[:, :, :CHUNK]
    for lvl in range(6):
        xb = x.astype(BF16)
        if lvl < 5:
            res = _bmm(xb, jnp.concatenate([zz.astype(BF16), xb], axis=-1))
            zz = zz + res[:, :, :2 * N_B]
            x = res[:, :, 2 * N_B:]
        else:
            zz = zz + _bmm(xb, zz.astype(BF16))
    w_m, u0 = zz[:, :, :N_B].astype(BF16), zz[:, :, N_B:].astype(BF16)
    res2 = _bmm(jnp.concatenate([a_bot[:, :, :CHUNK], bhkh_t[:, :, :CHUNK]], axis=1), w_m)
    q_hat = rt32 + res2[:, :CHUNK]
    eye = lax.broadcasted_iota(jnp.int32, (1, N_B, N_B), 1) == lax.broadcasted_iota(jnp.int32, (1, N_B, N_B), 2)
    m_t = jnp.where(eye, pend3, 0.0) + res2[:, CHUNK:]
    lhs = jnp.concatenate([jnp.concatenate([a_bot, q_hat.astype(BF16)], axis=-1),
                           jnp.concatenate([bhkh_t, m_t.astype(BF16)], axis=-1)], axis=1)
    rhs = jnp.concatenate([u0, vb, st_scr[...].astype(BF16)], axis=1)
    res3 = _bmm(lhs, rhs)
    y3 = res3[:, :CHUNK]
    st_scr[...] = res3[:, CHUNK:]

    mean = jnp.mean(y3, axis=-1, keepdims=True)
    var = jnp.mean(jnp.square(y3 - mean), axis=-1, keepdims=True)
    y3 = (y3 - mean) * lax.rsqrt(var + GN_EPS) * per_head(gng_ref) + per_head(gnb_ref)
    y3 = y3 + jnp.sum(r3 * k3 * per_head(rk_ref), axis=-1, keepdims=True) * v3
    for s in range(sb):
        y2 = jnp.concatenate([y3[s * H_B + h] for h in range(H_B)], axis=-1)
        y_ref[s] = (y2 * g[s * CHUNK:(s + 1) * CHUNK]).astype(y_ref.dtype)

    @pl.when(c == pl.num_programs(1) - 1)
    def _():
        so_ref[...] = jnp.swapaxes(st_scr[...], 1, 2).reshape(sb, H_B, N_B, N_B)


def wkv_group(pb, l, ls, shift0, s0, mu, w0, w_up, a0, a_up, g_up, k_k, k_a, r_k, gn_g, gn_b):
    b, t, _ = pb.shape
    sb = min(WKV_STREAMS, b)
    small = [mu, w0, w_up, a0, a_up, g_up, k_k, k_a, r_k, gn_g, gn_b]
    return pl.pallas_call(
        functools.partial(_wkv_kernel, sb=sb),
        grid=(b // sb, t // CHUNK),
        in_specs=[pl.BlockSpec((sb, CHUNK, D_SHIFT), lambda bi, c: (bi, c, 0)),
                  pl.BlockSpec((None, sb, 1, D_SHIFT), lambda bi, c: (ls, bi, 0, 0)),
                  pl.BlockSpec((None, sb, H_B, N_B, N_B), lambda bi, c: (ls, bi, 0, 0, 0))]
        + [_layer_spec(a, l) for a in small],
        out_specs=[pl.BlockSpec((sb, CHUNK, D_B), lambda bi, c: (bi, c, 0)),
                   pl.BlockSpec((sb, H_B, N_B, N_B), lambda bi, c: (bi, 0, 0, 0))],
        out_shape=[jax.ShapeDtypeStruct((b, t, D_B), BF16), jax.ShapeDtypeStruct((b, H_B, N_B, N_B), F32)],
        scratch_shapes=[pltpu.VMEM((sb * H_B, N_B, N_B), F32), pltpu.VMEM((sb, 1, D_SHIFT), F32)],
        compiler_params=_cparams("parallel", "arbitrary"),
        name="wkv_group",
    )(pb, shift0, s0, *small)


def _out_kernel(x_ref, ya_ref, yb_ref, wo_ref, g_ref, wq_ref, mk_ref, mv_ref, wmo_ref, o_ref, *, spt):
    x = x_ref[...] + _dot(ya_ref[...], wo_ref[:H_A * DV, :]) + _dot(yb_ref[...], wo_ref[H_A * DV:, :])
    q = _dot(_rms(x, g_ref[...]).astype(BF16), wq_ref[...])
    ts = x.shape[0] // spt
    rows = []
    for s in range(spt):
        heads = []
        for hd in range(MEM_HEADS):
            sl = slice(hd * MEM_HD, (hd + 1) * MEM_HD)
            sc = _dot_nt(q[s * ts:(s + 1) * ts, sl].astype(BF16), mk_ref[s, :, sl].astype(BF16)) / math.sqrt(MEM_HD)
            sc = jnp.exp(sc - jnp.max(sc, axis=-1, keepdims=True))
            p = sc / jnp.sum(sc, axis=-1, keepdims=True)
            heads.append(_dot(p.astype(BF16), mv_ref[s, :, sl].astype(BF16)))
        rows.append(jnp.concatenate(heads, axis=-1))
    att = rows[0] if spt == 1 else jnp.concatenate(rows, axis=0)
    o_ref[...] = x + _dot(att.astype(BF16), wmo_ref[...])


def mixer_out_xattn(x, ya, yb, l, w_out, g, w_mq, lm, mk, mv, w_mo, *, t_stream):
    n, d = x.shape
    tm = min(TOKEN_TILE, n)
    spt = max(1, tm // t_stream)
    tps = max(1, t_stream // tm)
    tok = lambda w: pl.BlockSpec((tm, w), lambda i: (i, 0))
    mem = pl.BlockSpec((None, spt, N_MEM, d), lambda i: (lm, i // tps, 0, 0))
    return pl.pallas_call(
        functools.partial(_out_kernel, spt=spt),
        grid=(n // tm,),
        in_specs=[tok(d), tok(H_A * DV), tok(D_B), _layer_spec(w_out, l), _layer_spec(g, l),
                  _layer_spec(w_mq, l), mem, mem, _layer_spec(w_mo, l)],
        out_specs=tok(d),
        out_shape=jax.ShapeDtypeStruct((n, d), F32),
        compiler_params=_cparams("parallel"),
        name="mixer_out_xattn",
    )(x, ya, yb, w_out, g, w_mq, mk, mv, w_mo)


def _memkv_kernel(m_ref, g_ref, wk_ref, wv_ref, k_ref, v_ref):
    m = _rms(m_ref[...], g_ref[...]).astype(BF16)
    k_ref[...] = _dot(m, wk_ref[...])
    v_ref[...] = _dot(m, wv_ref[...])


def mem_kv(mem, l, g, w_mk, w_mv):
    n, d = mem.shape
    tm = min(TOKEN_TILE, n)
    tok = pl.BlockSpec((tm, d), lambda i: (i, 0))
    return pl.pallas_call(
        _memkv_kernel,
        grid=(n // tm,),
        in_specs=[tok, _layer_spec(g, l), _layer_spec(w_mk, l), _layer_spec(w_mv, l)],
        out_specs=[tok, tok],
        out_shape=[jax.ShapeDtypeStruct((n, d), F32)] * 2,
        compiler_params=_cparams("parallel"),
        name="mem_kv",
    )(mem, g, w_mk, w_mv)


def _rot_half_cols(w):
    half = DR // 2
    shp = w.shape
    w = w.reshape(shp[:-1] + (shp[-1] // DR, 2, half))
    return jnp.concatenate([-w[..., 1:, :], w[..., :1, :]], axis=-2).reshape(shp)


def _rope_tables(pos, reps):
    half = DR // 2
    inv = ROPE_THETA ** (-jnp.arange(half, dtype=F32) / half)
    ang = pos.astype(F32)[:, None] * inv[None, :]
    cos = jnp.tile(jnp.cos(ang), (1, 2 * reps))
    sin = jnp.tile(jnp.sin(ang), (1, 2 * reps))
    return cos, sin


def _group_tables(pos, n_tokens):
    rows = max(pos.shape[0], min(TOKEN_TILE, n_tokens))
    pos = jnp.tile(pos, rows // pos.shape[0])
    return _rope_tables(pos, H_A) + _rope_tables(pos, 1)


def _layer_group(x, tabs, lw, l, b, t, past_ckv, past_kr, ls, wkv0, shift0, lm, mk, mv, *, tq, kb, final_norm):
    x = ffn_half(x, l, lw["ffn1_norm"], lw["ffn1_wg"], lw["ffn1_wu"], lw["ffn1_wd"], lw["final_norm"],
                 final_norm=False)
    q, ckv, kr, pb = mixer_proj(x, l, lw["mix_norm"], lw["w_in_ext"], lw["q_norm"], lw["w_uq_ext"], lw["w_uk"],
                                lw["kv_norm"], *tabs)
    ckv3, kr3 = ckv.reshape(b, t, KV_RANK), kr.reshape(b, t, DR)
    ya = mla_attention(q, ckv3, kr3, l, past_ckv, past_kr, lw["w_uv"], tq=tq, kb=kb)
    pb3 = pb.reshape(b, t, D_SHIFT)
    yb, wkv = wkv_group(pb3, l, ls, shift0, wkv0, *(lw[n] for n in (
        "shift_mu", "w0", "w_up", "a0", "a_up", "g_up", "k_k", "k_a", "r_k", "gn_gain", "gn_bias")))
    x = mixer_out_xattn(x, ya.reshape(b * t, H_A * DV), yb.reshape(b * t, D_B), l, lw["w_out"], lw["xattn_norm"],
                        lw["w_mq"], lm, mk, mv, lw["w_mo"], t_stream=t)
    x = ffn_half(x, l, lw["ffn2_norm"], lw["ffn2_wg"], lw["ffn2_wu"], lw["ffn2_wd"], lw["final_norm"],
                 final_norm=final_norm)
    return x, ckv3, kr3, wkv, pb3[:, -1:, :]


def kernel(x_prompt, x_sample, mem_prompt, cache_ckv, cache_krope, cache_mem_k, cache_mem_v, state_wkv, state_shift, ffn1_norm, ffn1_w_gate, ffn1_w_up, ffn1_w_down, mix_norm, w_in, q_norm, w_uq, kv_norm, w_uk, w_uv, shift_mu, w0, w_up, a0, a_up, g_up, k_k, k_a, r_k, gn_gain, gn_bias, w_out, xattn_norm, mem_kv_norm, w_mq, w_mk, w_mv, w_mo, ffn2_norm, ffn2_w_gate, ffn2_w_up, ffn2_w_down, final_norm):
    depth = w_in.shape[0]
    b_p, t_p, d = x_prompt.shape
    b_s, t_s, _ = x_sample.shape
    n_past = cache_ckv.shape[2]
    bf = lambda a: a.astype(BF16)
    vec = lambda a: a.reshape(depth, 1, -1)

    o = Q_RANK + KV_RANK
    w_kr = w_in[..., o:o + DR]
    w_in_ext = jnp.concatenate(
        [w_in[..., :o + DR], _rot_half_cols(w_kr), jnp.zeros(w_in.shape[:2] + (PROJ_HEAD - o - 2 * DR,), w_in.dtype),
         w_in[..., o + DR:]], axis=-1)
    uq = w_uq.reshape(depth, Q_RANK, H_A, DN + DR)
    uq_nope = uq[..., :DN].reshape(depth, Q_RANK, H_A * DN)
    uq_rope = uq[..., DN:].reshape(depth, Q_RANK, H_A * DR)
    w_uq_ext = jnp.concatenate([uq_nope, uq_rope, _rot_half_cols(uq_rope)], axis=-1)
    lw = dict(
        ffn1_norm=vec(ffn1_norm), ffn1_wg=bf(ffn1_w_gate), ffn1_wu=bf(ffn1_w_up), ffn1_wd=bf(ffn1_w_down),
        mix_norm=vec(mix_norm), w_in_ext=bf(w_in_ext), q_norm=vec(q_norm), w_uq_ext=bf(w_uq_ext),
        kv_norm=vec(kv_norm), w_uk=bf(w_uk), w_uv=bf(w_uv), shift_mu=vec(shift_mu), w0=vec(w0), w_up=bf(w_up),
        a0=vec(a0), a_up=bf(a_up), g_up=bf(g_up), k_k=vec(k_k), k_a=vec(k_a), r_k=vec(r_k), gn_gain=vec(gn_gain),
        gn_bias=vec(gn_bias), w_out=bf(w_out), xattn_norm=vec(xattn_norm), w_mq=bf(w_mq), w_mo=bf(w_mo),
        ffn2_norm=vec(ffn2_norm), ffn2_wg=bf(ffn2_w_gate), ffn2_wu=bf(ffn2_w_up), ffn2_wd=bf(ffn2_w_down),
        final_norm=final_norm.reshape(1, 1, d))
    mem_norm, w_mk_b, w_mv_b = vec(mem_kv_norm), bf(w_mk), bf(w_mv)

    tabs_p = _group_tables(jnp.arange(t_p, dtype=jnp.int32), b_p * t_p)
    tabs_s = _group_tables(n_past + jnp.arange(t_s, dtype=jnp.int32), b_s * t_s)
    wkv_zero = jnp.zeros((1, b_p, H_B, N_B, N_B), x_prompt.dtype)
    shift_zero = jnp.zeros((1, b_p, 1, D_SHIFT), x_prompt.dtype)
    mem_k_s = cache_mem_k.reshape(depth, b_s, N_MEM, d)
    mem_v_s = cache_mem_v.reshape(depth, b_s, N_MEM, d)
    kb_s = -(-(n_past + t_s) // 128) * 128

    xp = x_prompt.reshape(b_p * t_p, d)
    xs = x_sample.reshape(b_s * t_s, d)
    mem2 = mem_prompt.reshape(b_p * N_MEM, d)
    outs_p = [[] for _ in range(6)]
    outs_s = [[] for _ in range(4)]
    for l in range(depth):
        last = l == depth - 1
        mk, mv = mem_kv(mem2, l, mem_norm, w_mk_b, w_mv_b)
        xp, c1, k1, s1, h1 = _layer_group(xp, tabs_p, lw, l, b_p, t_p, None, None, 0, wkv_zero, shift_zero, 0,
                                          mk.reshape(1, b_p, N_MEM, d), mv.reshape(1, b_p, N_MEM, d),
                                          tq=128, kb=512, final_norm=last)
        for acc, val in zip(outs_p, (c1, k1, mk.reshape(b_p, N_MEM, MEM_HEADS, MEM_HD),
                                     mv.reshape(b_p, N_MEM, MEM_HEADS, MEM_HD), s1, h1)):
            acc.append(val)
        xs, c2, k2, s2, h2 = _layer_group(xs, tabs_s, lw, l, b_s, t_s, cache_ckv, cache_krope, l, state_wkv,
                                          state_shift, l, mem_k_s, mem_v_s, tq=t_s, kb=kb_s, final_norm=last)
        for acc, val in zip(outs_s, (c2, k2, s2, h2)):
            acc.append(val)
    stack = lambda seq: jnp.stack(seq)
    return (xp.reshape(b_p, t_p, d), xs.reshape(b_s, t_s, d),
            stack(outs_p[0]), stack(outs_p[1]), stack(outs_p[2]), stack(outs_p[3]), stack(outs_p[4]), stack(outs_p[5]),
            stack(outs_s[0]), stack(outs_s[1]), stack(outs_s[2]), stack(outs_s[3]))
```

```python
import functools
import math

import jax
import jax.numpy as jnp
from jax import lax
from jax.experimental import pallas as pl
from jax.experimental.pallas import tpu as pltpu

F32 = jnp.float32
BF16 = jnp.bfloat16

D_MODEL = 1024
CHUNK = 64
H_A, DN, DR, DV = 8, 64, 32, 64
Q_RANK, KV_RANK = 256, 128
DQK = KV_RANK + DR
ROPE_THETA = 10000.0
MLA_SCALE = 1.0 / math.sqrt(DN + DR)
LOG2E = math.log2(math.e)
H_B, N_B = 8, 64
D_B = H_B * N_B
W_RANK, A_RANK, G_RANK = 64, 64, 128
D_SHIFT = 3 * D_B + W_RANK + A_RANK + G_RANK
N_MEM, MEM_HEADS = 256, 4
MEM_HD = D_MODEL // MEM_HEADS
D_FF = 2816
EPS = 1e-6
GN_EPS = 64e-5
NEG_INF = -1e30

PROJ_HEAD = 512
D_PROJ = PROJ_HEAD + D_SHIFT

TOKEN_TILE = 256
WKV_STREAMS = 4
MLA_ROW_TILE = 512
VMEM_LIMIT = 56 * 1024 * 1024


def _cparams(*sem):
    return pltpu.CompilerParams(dimension_semantics=sem, vmem_limit_bytes=VMEM_LIMIT)


def _layer_spec(arr, l):
    nd = arr.ndim - 1
    return pl.BlockSpec((None,) + arr.shape[1:], lambda *_: (l,) + (0,) * nd, pipeline_mode=pl.Buffered(1))


def _rms(x, g):
    return x * lax.rsqrt(jnp.mean(x * x, axis=-1, keepdims=True) + EPS) * g


def _dot(a, b):
    return jnp.dot(a, b, preferred_element_type=F32)


def _dot_nt(a, b):
    return lax.dot_general(a, b, (((1,), (1,)), ((), ())), preferred_element_type=F32)


def _bmm(a, b):
    return jnp.einsum("gmk,gkn->gmn", a, b, preferred_element_type=F32)


def _bmm_nt(a, b):
    return jnp.einsum("gmk,gnk->gmn", a, b, preferred_element_type=F32)


def _ffn_kernel(x_ref, g_ref, wg_ref, wu_ref, wd_ref, gf_ref, o_ref, *, final_norm):
    x = x_ref[...]
    h = _rms(x, g_ref[...]).astype(BF16)
    gate = _dot(h, wg_ref[...])
    up = _dot(h, wu_ref[...])
    act = (gate * jax.nn.sigmoid(gate) * up).astype(BF16)
    y = x + 0.5 * _dot(act, wd_ref[...])
    if final_norm:
        y = _rms(y, gf_ref[...])
    o_ref[...] = y


def ffn_half(x, l, g, wg, wu, wd, gf, *, final_norm):
    n, d = x.shape
    tm = min(TOKEN_TILE, n)
    return pl.pallas_call(
        functools.partial(_ffn_kernel, final_norm=final_norm),
        grid=(n // tm,),
        in_specs=[pl.BlockSpec((tm, d), lambda i: (i, 0)), _layer_spec(g, l), _layer_spec(wg, l),
                  _layer_spec(wu, l), _layer_spec(wd, l), _layer_spec(gf, 0)],
        out_specs=pl.BlockSpec((tm, d), lambda i: (i, 0)),
        out_shape=jax.ShapeDtypeStruct((n, d), F32),
        compiler_params=_cparams("parallel"),
        name="ffn_half",
    )(x, g, wg, wu, wd, gf)


def _proj_kernel(x_ref, g_ref, win_ref, qn_ref, wuq_ref, wuk_ref, kvn_ref, cq_ref, sq_ref, ck_ref, sk_ref,
                 q_ref, ckv_ref, kr_ref, pb_ref):
    h = _rms(x_ref[...], g_ref[...]).astype(BF16)
    proj = _dot(h, win_ref[...])
    c_q = proj[:, :Q_RANK]
    c_kv = proj[:, Q_RANK:Q_RANK + KV_RANK]
    o = Q_RANK + KV_RANK
    k_r, k_rot = proj[:, o:o + DR], proj[:, o + DR:o + 2 * DR]
    ckv_ref[...] = _rms(c_kv, kvn_ref[...])
    kr_ref[...] = k_r * ck_ref[...] + k_rot * sk_ref[...]
    pb_ref[...] = proj[:, PROJ_HEAD:]
    qf = _dot(_rms(c_q, qn_ref[...]).astype(BF16), wuq_ref[...])
    nq = H_A * DN
    q_rope = (qf[:, nq:nq + H_A * DR] * cq_ref[...] + qf[:, nq + H_A * DR:] * sq_ref[...]) * (MLA_SCALE * LOG2E)
    for hd in range(H_A):
        q_lat = _dot_nt(qf[:, hd * DN:(hd + 1) * DN].astype(BF16), wuk_ref[hd]) * (MLA_SCALE * LOG2E)
        q_ref[hd, :, :KV_RANK] = q_lat.astype(BF16)
        q_ref[hd, :, KV_RANK:] = q_rope[:, hd * DR:(hd + 1) * DR].astype(BF16)


def mixer_proj(x, l, g, w_in_ext, q_norm, w_uq_ext, w_uk, kv_norm, cos_q, sin_q, cos_k, sin_k):
    n, d = x.shape
    tm = min(TOKEN_TILE, n)
    n_tab = cos_q.shape[0] // tm
    tok = lambda w: pl.BlockSpec((tm, w), lambda i: (i, 0))
    tab = lambda w: pl.BlockSpec((tm, w), lambda i: (i % n_tab, 0))
    return pl.pallas_call(
        _proj_kernel,
        grid=(n // tm,),
        in_specs=[tok(d), _layer_spec(g, l), _layer_spec(w_in_ext, l), _layer_spec(q_norm, l),
                  _layer_spec(w_uq_ext, l), _layer_spec(w_uk, l), _layer_spec(kv_norm, l),
                  tab(H_A * DR), tab(H_A * DR), tab(DR), tab(DR)],
        out_specs=[pl.BlockSpec((H_A, tm, DQK), lambda i: (0, i, 0)), tok(KV_RANK), tok(DR), tok(D_SHIFT)],
        out_shape=[jax.ShapeDtypeStruct((H_A, n, DQK), BF16), jax.ShapeDtypeStruct((n, KV_RANK), F32),
                   jax.ShapeDtypeStruct((n, DR), F32), jax.ShapeDtypeStruct((n, D_SHIFT), F32)],
        compiler_params=_cparams("parallel"),
        name="mixer_proj",
    )(x, g, w_in_ext, q_norm, w_uq_ext, w_uk, kv_norm, cos_q, sin_q, cos_k, sin_k)


def _mla_kernel(*refs, tq, kb, n_past, t_new, has_past):
    if has_past:
        q_ref, ckv_ref, kr_ref, pckv_ref, pkr_ref, wuvt_ref, o_ref, kcat, vt = refs
    else:
        q_ref, ckv_ref, kr_ref, wuvt_ref, o_ref, kcat, vt = refs
    i = pl.program_id(1)
    t_all = n_past + t_new
    t_pad = kcat.shape[0]

    @pl.when(i == 0)
    def _():
        if t_pad > t_all:
            kcat[t_all:, :] = jnp.zeros((t_pad - t_all, DQK), BF16)
            vt[:, t_all:] = jnp.zeros((KV_RANK, t_pad - t_all), BF16)
        pieces = [(0, pckv_ref, pkr_ref, n_past)] if has_past else []
        for base, c_ref, r_ref, n in pieces + [(n_past, ckv_ref, kr_ref, t_new)]:
            kcat[base:base + n, :KV_RANK] = c_ref[0].astype(BF16)
            kcat[base:base + n, KV_RANK:] = r_ref[0].astype(BF16)
            step_t = min(n, 512)
            for o in range(0, n, step_t):
                vt[:, base + o:base + o + step_t] = c_ref[0, o:o + step_t, :].T.astype(BF16)

    rows = H_A * tq
    q = q_ref[...].reshape(rows, DQK)
    t_in = lax.broadcasted_iota(jnp.int32, (1, rows), 1) & (tq - 1)
    limit = ((n_past + i * tq + t_in) // CHUNK + 1) * CHUNK
    n_blocks = (n_past + (i + 1) * tq + kb - 1) // kb

    def scores(j):
        return _dot_nt(kcat[pl.ds(pl.multiple_of(j * kb, kb), kb), :], q)

    def update(j, carry, st, masked):
        m, l, acc = carry
        if masked:
            k_pos = j * kb + lax.broadcasted_iota(jnp.int32, (kb, 1), 0)
            st = jnp.where(k_pos < limit, st, NEG_INF)
        m_new = jnp.maximum(m, jnp.max(st, axis=0, keepdims=True))
        alpha = jnp.exp2(m - m_new)
        p = jnp.exp2(st - m_new)
        l = alpha * l + jnp.sum(p, axis=0, keepdims=True)
        acc = alpha * acc + _dot(vt[:, pl.ds(pl.multiple_of(j * kb, kb), kb)], p.astype(BF16))
        return m_new, l, acc

    def pair(jj, carry):
        st0, st1 = scores(2 * jj), scores(2 * jj + 1)
        return update(2 * jj + 1, update(2 * jj, carry, st0, False), st1, False)

    def single(j, carry):
        return update(j, carry, scores(j), False)

    m0 = jnp.full((1, rows), NEG_INF, F32)
    l0 = jnp.zeros((1, rows), F32)
    a0 = jnp.zeros((KV_RANK, rows), F32)
    n_pairs = (n_blocks - 1) // 2
    carry = lax.fori_loop(0, n_pairs, pair, (m0, l0, a0))
    carry = lax.fori_loop(2 * n_pairs, n_blocks - 1, single, carry)
    _, l, acc = update(n_blocks - 1, carry, scores(n_blocks - 1), True)
    o_lat = (acc / l).astype(BF16)
    outs = [_dot(wuvt_ref[hd], o_lat[:, hd * tq:(hd + 1) * tq]) for hd in range(H_A)]
    o_ref[0] = jnp.concatenate(outs, axis=0).T.astype(o_ref.dtype)


def mla_attention(q, ckv, kr, l, past_ckv, past_kr, w_uv_t, *, tq, kb):
    b, t, _ = ckv.shape
    has_past = past_ckv is not None
    n_past = past_ckv.shape[2] if has_past else 0
    t_pad = -(-(n_past + t) // kb) * kb
    nq = t // tq
    for i in range(nq):
        first_limit = ((n_past + i * tq) // CHUNK + 1) * CHUNK
        assert first_limit // kb >= -(-(n_past + (i + 1) * tq) // kb) - 1, (tq, kb, n_past, i)
    stream = lambda n, w: pl.BlockSpec((1, n, w), lambda bi, i: (bi, 0, 0))
    past = lambda w: pl.BlockSpec((None, 1, n_past, w), lambda bi, i: (l, bi, 0, 0))
    in_specs = [pl.BlockSpec((H_A, tq, DQK), lambda bi, i: (0, bi * nq + i, 0)), stream(t, KV_RANK), stream(t, DR)]
    args = [q, ckv, kr]
    if has_past:
        in_specs += [past(KV_RANK), past(DR)]
        args += [past_ckv, past_kr]
    in_specs.append(_layer_spec(w_uv_t, l))
    args.append(w_uv_t)
    return pl.pallas_call(
        functools.partial(_mla_kernel, tq=tq, kb=kb, n_past=n_past, t_new=t, has_past=has_past),
        grid=(b, nq),
        in_specs=in_specs,
        out_specs=pl.BlockSpec((1, tq, H_A * DV), lambda bi, i: (bi, i, 0)),
        out_shape=jax.ShapeDtypeStruct((b, t, H_A * DV), BF16),
        scratch_shapes=[pltpu.VMEM((t_pad, DQK), BF16), pltpu.VMEM((KV_RANK, t_pad), BF16)],
        compiler_params=_cparams("parallel", "arbitrary"),
        name="mla_attention",
    )(*args)


def _split3(x):
    hi = x.astype(BF16)
    r1 = x - hi.astype(F32)
    mid = r1.astype(BF16)
    lo = (r1 - mid.astype(F32)).astype(BF16)
    return hi, mid, lo


def _wkv_kernel(pb_ref, sh0_ref, s0_ref, mu_ref, w0_ref, wup_ref, a0_ref, aup_ref, gup_ref, kk_ref, ka_ref,
                rk_ref, gng_ref, gnb_ref, y_ref, so_ref, st_scr, prev_scr, *, sb):
    c = pl.program_id(1)
    nb = sb * H_B
    n = sb * CHUNK

    @pl.when(c == 0)
    def _():
        st_scr[...] = jnp.swapaxes(s0_ref[...].reshape(nb, N_B, N_B), 1, 2)
        prev_scr[...] = sh0_ref[...]

    row = lax.broadcasted_iota(jnp.int32, (CHUNK, 1), 0)
    parts = []
    for s in range(sb):
        pb = pb_ref[s]
        prev = jnp.where(row == 0, prev_scr[s], pltpu.roll(pb, shift=1, axis=0))
        prev_scr[s] = pb[CHUNK - 1:, :]
        parts.append(pb + mu_ref[...] * (prev - pb))
    xs = jnp.concatenate(parts, axis=0)
    r = xs[:, :D_B]
    k = xs[:, D_B:2 * D_B]
    v = xs[:, 2 * D_B:3 * D_B]
    o = 3 * D_B
    xw, xa, xg = xs[:, o:o + W_RANK], xs[:, o + W_RANK:o + W_RANK + A_RANK], xs[:, o + W_RANK + A_RANK:]
    z = w0_ref[...] + _dot(jnp.tanh(xw).astype(BF16), wup_ref[...])
    logw = -math.exp(-0.5) * jax.nn.sigmoid(z)
    gate = jax.nn.sigmoid(a0_ref[...] + _dot(xa.astype(BF16), aup_ref[...]))
    g = _dot(jax.nn.sigmoid(xg).astype(BF16), gup_ref[...])
    kkf = k * kk_ref[...]
    k2 = k * (1.0 + (gate - 1.0) * ka_ref[...])
    ri = lax.broadcasted_iota(jnp.int32, (n, 1), 0)
    ci = lax.broadcasted_iota(jnp.int32, (1, n), 1)
    same_stream = jnp.right_shift(ri, 6) == jnp.right_shift(ci, 6)
    ones_tri = jnp.where(ri >= ci, jnp.where(same_stream, 1.0, 0.0), 0.0).astype(BF16)
    cs = sum(_dot(ones_tri, piece) for piece in _split3(logw))
    p_in = jnp.exp(cs)
    p_prev = jnp.exp(cs - logw)
    p_inv = jnp.exp(-cs)

    def heads3(x):
        return jnp.stack([x[s * CHUNK:(s + 1) * CHUNK, h * N_B:(h + 1) * N_B]
                          for s in range(sb) for h in range(H_B)], axis=0)

    def per_head(ref):
        return jnp.stack([ref[:, h * N_B:(h + 1) * N_B] for _ in range(sb) for h in range(H_B)], axis=0)

    r3, k3, v3 = heads3(r), heads3(k2), heads3(v)
    kkf3, kg3 = heads3(kkf), heads3(kkf * gate)
    pin3, pprev3, pinv3 = heads3(p_in), heads3(p_prev), heads3(p_inv)
    rs = lax.rsqrt(jnp.sum(kkf3 * kkf3, axis=-1, keepdims=True) + 1e-12)
    pend3 = pin3[:, CHUNK - 1:, :]
    at32 = -(kkf3 * rs) * pprev3
    kt32 = k3 * pinv3
    bt32 = kg3 * rs * pinv3
    rt32 = r3 * pin3
    at, rt = at32.astype(BF16), rt32.astype(BF16)
    bk = jnp.concatenate([bt32, kt32], axis=1).astype(BF16)
    bhkh_t = jnp.swapaxes(jnp.concatenate([bt32 * pend3, kt32 * pend3], axis=1), 1, 2).astype(BF16)
    vb = v3.astype(BF16)

    tt = lax.broadcasted_iota(jnp.int32, (1, CHUNK, 2 * CHUNK), 1)
    ss = lax.broadcasted_iota(jnp.int32, (1, CHUNK, 2 * CHUNK), 2) & (CHUNK - 1)
    a_top = jnp.where(tt > ss, _bmm_nt(at, bk), 0.0)
    a_bot = jnp.where(tt >= ss, _bmm_nt(rt, bk), 0.0).astype(BF16)
    zero_v = jnp.concatenate([jnp.zeros((nb, CHUNK, N_B), BF16), vb], axis=1)
    zz = jnp.concatenate([at32, _bmm(a_top.astype(BF16), zero_v)], axis=-1)
    x = a_top[:, :, :CHUNK]
    for lvl in range(6):
        xb = x.astype(BF16)
        if lvl < 5:
            res = _bmm(xb, jnp.concatenate([zz.astype(BF16), xb], axis=-1))
            zz = zz + res[:, :, :2 * N_B]
            x = res[:, :, 2 * N_B:]
        else:
            zz = zz + _bmm(xb, zz.astype(BF16))
    w_m, u0 = zz[:, :, :N_B].astype(BF16), zz[:, :, N_B:].astype(BF16)
    res2 = _bmm(jnp.concatenate([a_bot[:, :, :CHUNK], bhkh_t[:, :, :CHUNK]], axis=1), w_m)
    q_hat = rt32 + res2[:, :CHUNK]
    eye = lax.broadcasted_iota(jnp.int32, (1, N_B, N_B), 1) == lax.broadcasted_iota(jnp.int32, (1, N_B, N_B), 2)
    m_t = jnp.where(eye, pend3, 0.0) + res2[:, CHUNK:]
    lhs = jnp.concatenate([jnp.concatenate([a_bot, q_hat.astype(BF16)], axis=-1),
                           jnp.concatenate([bhkh_t, m_t.astype(BF16)], axis=-1)], axis=1)
    rhs = jnp.concatenate([u0, vb, st_scr[...].astype(BF16)], axis=1)
    res3 = _bmm(lhs, rhs)
    y3 = res3[:, :CHUNK]
    st_scr[...] = res3[:, CHUNK:]

    mean = jnp.mean(y3, axis=-1, keepdims=True)
    var = jnp.mean(jnp.square(y3 - mean), axis=-1, keepdims=True)
    y3 = (y3 - mean) * lax.rsqrt(var + GN_EPS) * per_head(gng_ref) + per_head(gnb_ref)
    y3 = y3 + jnp.sum(r3 * k3 * per_head(rk_ref), axis=-1, keepdims=True) * v3
    for s in range(sb):
        y2 = jnp.concatenate([y3[s * H_B + h] for h in range(H_B)], axis=-1)
        y_ref[s] = (y2 * g[s * CHUNK:(s + 1) * CHUNK]).astype(y_ref.dtype)

    @pl.when(c == pl.num_programs(1) - 1)
    def _():
        so_ref[...] = jnp.swapaxes(st_scr[...], 1, 2).reshape(sb, H_B, N_B, N_B)


def wkv_group(pb, l, ls, shift0, s0, mu, w0, w_up, a0, a_up, g_up, k_k, k_a, r_k, gn_g, gn_b):
    b, t, _ = pb.shape
    sb = min(WKV_STREAMS, b)
    small = [mu, w0, w_up, a0, a_up, g_up, k_k, k_a, r_k, gn_g, gn_b]
    return pl.pallas_call(
        functools.partial(_wkv_kernel, sb=sb),
        grid=(b // sb, t // CHUNK),
        in_specs=[pl.BlockSpec((sb, CHUNK, D_SHIFT), lambda bi, c: (bi, c, 0)),
                  pl.BlockSpec((None, sb, 1, D_SHIFT), lambda bi, c: (ls, bi, 0, 0)),
                  pl.BlockSpec((None, sb, H_B, N_B, N_B), lambda bi, c: (ls, bi, 0, 0, 0))]
        + [_layer_spec(a, l) for a in small],
        out_specs=[pl.BlockSpec((sb, CHUNK, D_B), lambda bi, c: (bi, c, 0)),
                   pl.BlockSpec((sb, H_B, N_B, N_B), lambda bi, c: (bi, 0, 0, 0))],
        out_shape=[jax.ShapeDtypeStruct((b, t, D_B), BF16), jax.ShapeDtypeStruct((b, H_B, N_B, N_B), F32)],
        scratch_shapes=[pltpu.VMEM((sb * H_B, N_B, N_B), F32), pltpu.VMEM((sb, 1, D_SHIFT), F32)],
        compiler_params=_cparams("parallel", "arbitrary"),
        name="wkv_group",
    )(pb, shift0, s0, *small)


def _out_kernel(x_ref, ya_ref, yb_ref, wo_ref, g_ref, wq_ref, mk_ref, mv_ref, wmo_ref, o_ref, *, spt):
    x = x_ref[...] + _dot(ya_ref[...], wo_ref[:H_A * DV, :]) + _dot(yb_ref[...], wo_ref[H_A * DV:, :])
    q = _dot(_rms(x, g_ref[...]).astype(BF16), wq_ref[...])
    ts = x.shape[0] // spt
    rows = []
    for s in range(spt):
        heads = []
        for hd in range(MEM_HEADS):
            sl = slice(hd * MEM_HD, (hd + 1) * MEM_HD)
            sc = _dot_nt(q[s * ts:(s + 1) * ts, sl].astype(BF16), mk_ref[s, :, sl].astype(BF16)) / math.sqrt(MEM_HD)
            sc = jnp.exp(sc - jnp.max(sc, axis=-1, keepdims=True))
            p = sc / jnp.sum(sc, axis=-1, keepdims=True)
            heads.append(_dot(p.astype(BF16), mv_ref[s, :, sl].astype(BF16)))
        rows.append(jnp.concatenate(heads, axis=-1))
    att = rows[0] if spt == 1 else jnp.concatenate(rows, axis=0)
    o_ref[...] = x + _dot(att.astype(BF16), wmo_ref[...])


def mixer_out_xattn(x, ya, yb, l, w_out, g, w_mq, lm, mk, mv, w_mo, *, t_stream):
    n, d = x.shape
    tm = min(TOKEN_TILE, n)
    spt = max(1, tm // t_stream)
    tps = max(1, t_stream // tm)
    tok = lambda w: pl.BlockSpec((tm, w), lambda i: (i, 0))
    mem = pl.BlockSpec((None, spt, N_MEM, d), lambda i: (lm, i // tps, 0, 0))
    return pl.pallas_call(
        functools.partial(_out_kernel, spt=spt),
        grid=(n // tm,),
        in_specs=[tok(d), tok(H_A * DV), tok(D_B), _layer_spec(w_out, l), _layer_spec(g, l),
                  _layer_spec(w_mq, l), mem, mem, _layer_spec(w_mo, l)],
        out_specs=tok(d),
        out_shape=jax.ShapeDtypeStruct((n, d), F32),
        compiler_params=_cparams("parallel"),
        name="mixer_out_xattn",
    )(x, ya, yb, w_out, g, w_mq, mk, mv, w_mo)


def _memkv_kernel(m_ref, g_ref, wk_ref, wv_ref, k_ref, v_ref):
    m = _rms(m_ref[...], g_ref[...]).astype(BF16)
    k_ref[...] = _dot(m, wk_ref[...])
    v_ref[...] = _dot(m, wv_ref[...])


def mem_kv(mem, l, g, w_mk, w_mv):
    n, d = mem.shape
    tm = min(TOKEN_TILE, n)
    tok = pl.BlockSpec((tm, d), lambda i: (i, 0))
    return pl.pallas_call(
        _memkv_kernel,
        grid=(n // tm,),
        in_specs=[tok, _layer_spec(g, l), _layer_spec(w_mk, l), _layer_spec(w_mv, l)],
        out_specs=[tok, tok],
        out_shape=[jax.ShapeDtypeStruct((n, d), F32)] * 2,
        compiler_params=_cparams("parallel"),
        name="mem_kv",
    )(mem, g, w_mk, w_mv)


def _rot_half_cols(w):
    half = DR // 2
    shp = w.shape
    w = w.reshape(shp[:-1] + (shp[-1] // DR, 2, half))
    return jnp.concatenate([-w[..., 1:, :], w[..., :1, :]], axis=-2).reshape(shp)


def _rope_tables(pos, reps):
    half = DR // 2
    inv = ROPE_THETA ** (-jnp.arange(half, dtype=F32) / half)
    ang = pos.astype(F32)[:, None] * inv[None, :]
    cos = jnp.tile(jnp.cos(ang), (1, 2 * reps))
    sin = jnp.tile(jnp.sin(ang), (1, 2 * reps))
    return cos, sin


def _group_tables(pos, n_tokens):
    rows = max(pos.shape[0], min(TOKEN_TILE, n_tokens))
    pos = jnp.tile(pos, rows // pos.shape[0])
    return _rope_tables(pos, H_A) + _rope_tables(pos, 1)


def _layer_group(x, tabs, lw, l, b, t, past_ckv, past_kr, ls, wkv0, shift0, lm, mk, mv, *, tq, kb, final_norm):
    x = ffn_half(x, l, lw["ffn1_norm"], lw["ffn1_wg"], lw["ffn1_wu"], lw["ffn1_wd"], lw["final_norm"],
                 final_norm=False)
    q, ckv, kr, pb = mixer_proj(x, l, lw["mix_norm"], lw["w_in_ext"], lw["q_norm"], lw["w_uq_ext"], lw["w_uk"],
                                lw["kv_norm"], *tabs)
    ckv3, kr3 = ckv.reshape(b, t, KV_RANK), kr.reshape(b, t, DR)
    ya = mla_attention(q, ckv3, kr3, l, past_ckv, past_kr, lw["w_uv_t"], tq=tq, kb=kb)
    pb3 = pb.reshape(b, t, D_SHIFT)
    yb, wkv = wkv_group(pb3, l, ls, shift0, wkv0, *(lw[n] for n in (
        "shift_mu", "w0", "w_up", "a0", "a_up", "g_up", "k_k", "k_a", "r_k", "gn_gain", "gn_bias")))
    x = mixer_out_xattn(x, ya.reshape(b * t, H_A * DV), yb.reshape(b * t, D_B), l, lw["w_out"], lw["xattn_norm"],
                        lw["w_mq"], lm, mk, mv, lw["w_mo"], t_stream=t)
    x = ffn_half(x, l, lw["ffn2_norm"], lw["ffn2_wg"], lw["ffn2_wu"], lw["ffn2_wd"], lw["final_norm"],
                 final_norm=final_norm)
    return x, ckv3, kr3, wkv, pb3[:, -1:, :]


def kernel(x_prompt, x_sample, mem_prompt, cache_ckv, cache_krope, cache_mem_k, cache_mem_v, state_wkv, state_shift, ffn1_norm, ffn1_w_gate, ffn1_w_up, ffn1_w_down, mix_norm, w_in, q_norm, w_uq, kv_norm, w_uk, w_uv, shift_mu, w0, w_up, a0, a_up, g_up, k_k, k_a, r_k, gn_gain, gn_bias, w_out, xattn_norm, mem_kv_norm, w_mq, w_mk, w_mv, w_mo, ffn2_norm, ffn2_w_gate, ffn2_w_up, ffn2_w_down, final_norm):
    depth = w_in.shape[0]
    b_p, t_p, d = x_prompt.shape
    b_s, t_s, _ = x_sample.shape
    n_past = cache_ckv.shape[2]
    bf = lambda a: a.astype(BF16)
    vec = lambda a: a.reshape(depth, 1, -1)

    o = Q_RANK + KV_RANK
    w_kr = w_in[..., o:o + DR]
    w_in_ext = jnp.concatenate(
        [w_in[..., :o + DR], _rot_half_cols(w_kr), jnp.zeros(w_in.shape[:2] + (PROJ_HEAD - o - 2 * DR,), w_in.dtype),
         w_in[..., o + DR:]], axis=-1)
    uq = w_uq.reshape(depth, Q_RANK, H_A, DN + DR)
    uq_nope = uq[..., :DN].reshape(depth, Q_RANK, H_A * DN)
    uq_rope = uq[..., DN:].reshape(depth, Q_RANK, H_A * DR)
    w_uq_ext = jnp.concatenate([uq_nope, uq_rope, _rot_half_cols(uq_rope)], axis=-1)
    lw = dict(
        ffn1_norm=vec(ffn1_norm), ffn1_wg=bf(ffn1_w_gate), ffn1_wu=bf(ffn1_w_up), ffn1_wd=bf(ffn1_w_down),
        mix_norm=vec(mix_norm), w_in_ext=bf(w_in_ext), q_norm=vec(q_norm), w_uq_ext=bf(w_uq_ext),
        kv_norm=vec(kv_norm), w_uk=bf(w_uk), w_uv_t=bf(jnp.swapaxes(w_uv, -1, -2)), shift_mu=vec(shift_mu), w0=vec(w0), w_up=bf(w_up),
        a0=vec(a0), a_up=bf(a_up), g_up=bf(g_up), k_k=vec(k_k), k_a=vec(k_a), r_k=vec(r_k), gn_gain=vec(gn_gain),
        gn_bias=vec(gn_bias), w_out=bf(w_out), xattn_norm=vec(xattn_norm), w_mq=bf(w_mq), w_mo=bf(w_mo),
        ffn2_norm=vec(ffn2_norm), ffn2_wg=bf(ffn2_w_gate), ffn2_wu=bf(ffn2_w_up), ffn2_wd=bf(ffn2_w_down),
        final_norm=final_norm.reshape(1, 1, d))
    mem_norm, w_mk_b, w_mv_b = vec(mem_kv_norm), bf(w_mk), bf(w_mv)

    tabs_p = _group_tables(jnp.arange(t_p, dtype=jnp.int32), b_p * t_p)
    tabs_s = _group_tables(n_past + jnp.arange(t_s, dtype=jnp.int32), b_s * t_s)
    wkv_zero = jnp.zeros((1, b_p, H_B, N_B, N_B), x_prompt.dtype)
    shift_zero = jnp.zeros((1, b_p, 1, D_SHIFT), x_prompt.dtype)
    mem_k_s = cache_mem_k.reshape(depth, b_s, N_MEM, d)
    mem_v_s = cache_mem_v.reshape(depth, b_s, N_MEM, d)
    kb_s = -(-(n_past + t_s) // (3 * 128)) * 128

    xp = x_prompt.reshape(b_p * t_p, d)
    xs = x_sample.reshape(b_s * t_s, d)
    mem2 = mem_prompt.reshape(b_p * N_MEM, d)
    outs_p = [[] for _ in range(6)]
    outs_s = [[] for _ in range(4)]
    for l in range(depth):
        last = l == depth - 1
        mk, mv = mem_kv(mem2, l, mem_norm, w_mk_b, w_mv_b)
        xp, c1, k1, s1, h1 = _layer_group(xp, tabs_p, lw, l, b_p, t_p, None, None, 0, wkv_zero, shift_zero, 0,
                                          mk.reshape(1, b_p, N_MEM, d), mv.reshape(1, b_p, N_MEM, d),
                                          tq=128, kb=512, final_norm=last)
        for acc, val in zip(outs_p, (c1, k1, mk.reshape(b_p, N_MEM, MEM_HEADS, MEM_HD),
                                     mv.reshape(b_p, N_MEM, MEM_HEADS, MEM_HD), s1, h1)):
            acc.append(val)
        xs, c2, k2, s2, h2 = _layer_group(xs, tabs_s, lw, l, b_s, t_s, cache_ckv, cache_krope, l, state_wkv,
                                          state_shift, l, mem_k_s, mem_v_s, tq=t_s, kb=kb_s, final_norm=last)
        for acc, val in zip(outs_s, (c2, k2, s2, h2)):
            acc.append(val)
    stack = lambda seq: jnp.stack(seq)
    return (xp.reshape(b_p, t_p, d), xs.reshape(b_s, t_s, d),
            stack(outs_p[0]), stack(outs_p[1]), stack(outs_p[2]), stack(outs_p[3]), stack(outs_p[4]), stack(outs_p[5]),
            stack(outs_s[0]), stack(outs_s[1]), stack(outs_s[2]), stack(outs_s[3]))
```

```python
import functools
import math

import jax
import jax.numpy as jnp
from jax import lax
from jax.experimental import pallas as pl
from jax.experimental.pallas import tpu as pltpu

F32 = jnp.float32
BF16 = jnp.bfloat16

D_MODEL = 1024
CHUNK = 64
H_A, DN, DR, DV = 8, 64, 32, 64
Q_RANK, KV_RANK = 256, 128
DQK = KV_RANK + DR
ROPE_THETA = 10000.0
MLA_SCALE = 1.0 / math.sqrt(DN + DR)
LOG2E = math.log2(math.e)
H_B, N_B = 8, 64
D_B = H_B * N_B
W_RANK, A_RANK, G_RANK = 64, 64, 128
D_SHIFT = 3 * D_B + W_RANK + A_RANK + G_RANK
N_MEM, MEM_HEADS = 256, 4
MEM_HD = D_MODEL // MEM_HEADS
MEM_LANE_HALVES = MEM_HD // 128
MEM_ROWS = MEM_LANE_HALVES * MEM_HEADS
D_FF = 2816
EPS = 1e-6
GN_EPS = 64e-5
NEG_INF = -1e30

PROJ_HEAD = 512
D_PROJ = PROJ_HEAD + D_SHIFT

TOKEN_TILE = 512
WKV_STREAMS = 4
MLA_ROW_TILE = 512
VMEM_LIMIT = 56 * 1024 * 1024


def _cparams(*sem):
    return pltpu.CompilerParams(dimension_semantics=sem, vmem_limit_bytes=VMEM_LIMIT)


def _layer_spec(arr, l):
    nd = arr.ndim - 1
    return pl.BlockSpec((None,) + arr.shape[1:], lambda *_: (l,) + (0,) * nd, pipeline_mode=pl.Buffered(1))


def _rms(x, g):
    return x * lax.rsqrt(jnp.mean(x * x, axis=-1, keepdims=True) + EPS) * g


def _dot(a, b):
    return jnp.dot(a, b, preferred_element_type=F32)


def _dot_nt(a, b):
    return lax.dot_general(a, b, (((1,), (1,)), ((), ())), preferred_element_type=F32)


def _bmm(a, b):
    return jnp.einsum("gmk,gkn->gmn", a, b, preferred_element_type=F32)


def _bmm_nt(a, b):
    return jnp.einsum("gmk,gnk->gmn", a, b, preferred_element_type=F32)


def _ffn_kernel(x_ref, g_ref, wg_ref, wu_ref, wd_ref, gf_ref, o_ref, *, final_norm):
    x = x_ref[...]
    h = _rms(x, g_ref[...]).astype(BF16)
    gate = _dot(h, wg_ref[...])
    up = _dot(h, wu_ref[...])
    act = (gate * jax.nn.sigmoid(gate) * up).astype(BF16)
    y = x + 0.5 * _dot(act, wd_ref[...])
    if final_norm:
        y = _rms(y, gf_ref[...])
    o_ref[...] = y


def ffn_half(x, l, g, wg, wu, wd, gf, *, final_norm):
    n, d = x.shape
    tm = min(TOKEN_TILE, n)
    return pl.pallas_call(
        functools.partial(_ffn_kernel, final_norm=final_norm),
        grid=(n // tm,),
        in_specs=[pl.BlockSpec((tm, d), lambda i: (i, 0)), _layer_spec(g, l), _layer_spec(wg, l),
                  _layer_spec(wu, l), _layer_spec(wd, l), _layer_spec(gf, 0)],
        out_specs=pl.BlockSpec((tm, d), lambda i: (i, 0)),
        out_shape=jax.ShapeDtypeStruct((n, d), F32),
        compiler_params=_cparams("parallel"),
        name="ffn_half",
    )(x, g, wg, wu, wd, gf)


def _proj_kernel(x_ref, g_ref, win_ref, qn_ref, wuq_ref, wuk_ref, kvn_ref, cq_ref, sq_ref, ck_ref, sk_ref,
                 q_ref, ckv_ref, kr_ref, pb_ref):
    h = _rms(x_ref[...], g_ref[...]).astype(BF16)
    proj = _dot(h, win_ref[...])
    c_q = proj[:, :Q_RANK]
    c_kv = proj[:, Q_RANK:Q_RANK + KV_RANK]
    o = Q_RANK + KV_RANK
    k_r, k_rot = proj[:, o:o + DR], proj[:, o + DR:o + 2 * DR]
    ckv_ref[...] = _rms(c_kv, kvn_ref[...])
    kr_ref[...] = k_r * ck_ref[...] + k_rot * sk_ref[...]
    pb_ref[...] = proj[:, PROJ_HEAD:]
    qf = _dot(_rms(c_q, qn_ref[...]).astype(BF16), wuq_ref[...])
    nq = H_A * DN
    q_rope = (qf[:, nq:nq + H_A * DR] * cq_ref[...] + qf[:, nq + H_A * DR:] * sq_ref[...]) * (MLA_SCALE * LOG2E)
    for hd in range(H_A):
        q_lat = _dot_nt(qf[:, hd * DN:(hd + 1) * DN].astype(BF16), wuk_ref[hd]) * (MLA_SCALE * LOG2E)
        q_ref[hd, :, :KV_RANK] = q_lat.astype(BF16)
        q_ref[hd, :, KV_RANK:] = q_rope[:, hd * DR:(hd + 1) * DR].astype(BF16)


def mixer_proj(x, l, g, w_in_ext, q_norm, w_uq_ext, w_uk, kv_norm, cos_q, sin_q, cos_k, sin_k):
    n, d = x.shape
    tm = min(TOKEN_TILE, n)
    n_tab = cos_q.shape[0] // tm
    tok = lambda w: pl.BlockSpec((tm, w), lambda i: (i, 0))
    tab = lambda w: pl.BlockSpec((tm, w), lambda i: (i % n_tab, 0))
    return pl.pallas_call(
        _proj_kernel,
        grid=(n // tm,),
        in_specs=[tok(d), _layer_spec(g, l), _layer_spec(w_in_ext, l), _layer_spec(q_norm, l),
                  _layer_spec(w_uq_ext, l), _layer_spec(w_uk, l), _layer_spec(kv_norm, l),
                  tab(H_A * DR), tab(H_A * DR), tab(DR), tab(DR)],
        out_specs=[pl.BlockSpec((H_A, tm, DQK), lambda i: (0, i, 0)), tok(KV_RANK), tok(DR), tok(D_SHIFT)],
        out_shape=[jax.ShapeDtypeStruct((H_A, n, DQK), BF16), jax.ShapeDtypeStruct((n, KV_RANK), F32),
                   jax.ShapeDtypeStruct((n, DR), F32), jax.ShapeDtypeStruct((n, D_SHIFT), F32)],
        compiler_params=_cparams("parallel"),
        name="mixer_proj",
    )(x, g, w_in_ext, q_norm, w_uq_ext, w_uk, kv_norm, cos_q, sin_q, cos_k, sin_k)


def _mla_kernel(*refs, tq, kb, n_past, t_new, has_past):
    if has_past:
        q_ref, ckv_ref, kr_ref, pckv_ref, pkr_ref, wuvt_ref, o_ref, kcat, vt = refs
    else:
        q_ref, ckv_ref, kr_ref, wuvt_ref, o_ref, kcat, vt = refs
        pckv_ref = pkr_ref = None
    i = pl.program_id(1)
    t_all = n_past + t_new
    t_pad = kcat.shape[0]

    @pl.when(i == 0)
    def _():
        if t_pad > t_all:
            kcat[t_all:, :] = jnp.zeros((t_pad - t_all, DQK), BF16)
            vt[:, t_all:] = jnp.zeros((KV_RANK, t_pad - t_all), BF16)
        pieces = [(0, pckv_ref, pkr_ref, n_past)] if has_past else []
        for base, c_ref, r_ref, n in pieces + [(n_past, ckv_ref, kr_ref, t_new)]:
            kcat[base:base + n, :KV_RANK] = c_ref[0].astype(BF16)
            step_t = min(n, 512)
            for o in range(0, n, step_t):
                vt[:, base + o:base + o + step_t] = c_ref[0, o:o + step_t, :].T.astype(BF16)
                if r_ref is pkr_ref:
                    kcat[base + o:base + o + step_t, KV_RANK:] = r_ref[0, :, o:o + step_t].T.astype(BF16)
            if r_ref is not pkr_ref:
                kcat[base:base + n, KV_RANK:] = r_ref[0].astype(BF16)

    rows = H_A * tq
    q = q_ref[...].reshape(rows, DQK)
    t_in = lax.broadcasted_iota(jnp.int32, (1, rows), 1) & (tq - 1)
    limit = ((n_past + i * tq + t_in) // CHUNK + 1) * CHUNK
    n_blocks = (n_past + (i + 1) * tq + kb - 1) // kb

    def scores(j):
        return _dot_nt(kcat[pl.ds(pl.multiple_of(j * kb, kb), kb), :], q)

    def update(j, carry, st, masked):
        m, l, acc = carry
        if masked:
            k_pos = j * kb + lax.broadcasted_iota(jnp.int32, (kb, 1), 0)
            st = jnp.where(k_pos < limit, st, NEG_INF)
        m_new = jnp.maximum(m, jnp.max(st, axis=0, keepdims=True))
        alpha = jnp.exp2(m - m_new)
        p = jnp.exp2(st - m_new)
        l = alpha * l + jnp.sum(p, axis=0, keepdims=True)
        acc = alpha * acc + _dot(vt[:, pl.ds(pl.multiple_of(j * kb, kb), kb)], p.astype(BF16))
        return m_new, l, acc

    def pair(jj, carry):
        st0, st1 = scores(2 * jj), scores(2 * jj + 1)
        return update(2 * jj + 1, update(2 * jj, carry, st0, False), st1, False)

    def single(j, carry):
        return update(j, carry, scores(j), False)

    m0 = jnp.full((1, rows), NEG_INF, F32)
    l0 = jnp.zeros((1, rows), F32)
    a0 = jnp.zeros((KV_RANK, rows), F32)
    n_pairs = (n_blocks - 1) // 2
    carry = lax.fori_loop(0, n_pairs, pair, (m0, l0, a0))
    carry = lax.fori_loop(2 * n_pairs, n_blocks - 1, single, carry)
    _, l, acc = update(n_blocks - 1, carry, scores(n_blocks - 1), True)
    o_lat = (acc / l).astype(BF16)
    outs = [_dot(wuvt_ref[hd], o_lat[:, hd * tq:(hd + 1) * tq]) for hd in range(H_A)]
    o_ref[0] = jnp.concatenate(outs, axis=0).T.astype(o_ref.dtype)


def mla_attention(q, ckv, kr, l, past_ckv, past_kr, w_uv_t, *, tq, kb):
    b, t, _ = ckv.shape
    has_past = past_ckv is not None
    n_past = past_ckv.shape[2] if has_past else 0
    t_pad = -(-(n_past + t) // kb) * kb
    nq = t // tq
    for i in range(nq):
        first_limit = ((n_past + i * tq) // CHUNK + 1) * CHUNK
        assert first_limit // kb >= -(-(n_past + (i + 1) * tq) // kb) - 1, (tq, kb, n_past, i)
    stream = lambda n, w: pl.BlockSpec((1, n, w), lambda bi, i: (bi, 0, 0))
    past_c = pl.BlockSpec((None, 1, n_past, KV_RANK), lambda bi, i: (l, bi, 0, 0))
    past_r = pl.BlockSpec((None, 1, DR, n_past), lambda bi, i: (l, bi, 0, 0))
    in_specs = [pl.BlockSpec((H_A, tq, DQK), lambda bi, i: (0, bi * nq + i, 0)), stream(t, KV_RANK), stream(t, DR)]
    args = [q, ckv, kr]
    if has_past:
        in_specs += [past_c, past_r]
        args += [past_ckv, past_kr]
    in_specs.append(_layer_spec(w_uv_t, l))
    args.append(w_uv_t)
    return pl.pallas_call(
        functools.partial(_mla_kernel, tq=tq, kb=kb, n_past=n_past, t_new=t, has_past=has_past),
        grid=(b, nq),
        in_specs=in_specs,
        out_specs=pl.BlockSpec((1, tq, H_A * DV), lambda bi, i: (bi, i, 0)),
        out_shape=jax.ShapeDtypeStruct((b, t, H_A * DV), BF16),
        scratch_shapes=[pltpu.VMEM((t_pad, DQK), BF16), pltpu.VMEM((KV_RANK, t_pad), BF16)],
        compiler_params=_cparams("parallel", "arbitrary"),
        name="mla_attention",
    )(*args)


def _split3(x):
    hi = x.astype(BF16)
    r1 = x - hi.astype(F32)
    mid = r1.astype(BF16)
    lo = (r1 - mid.astype(F32)).astype(BF16)
    return hi, mid, lo


def _wkv_kernel(pb_ref, sh0_ref, s0_ref, mu_ref, w0_ref, wup_ref, a0_ref, aup_ref, gup_ref, kk_ref, ka_ref,
                rk_ref, gng_ref, gnb_ref, y_ref, so_ref, st_scr, prev_scr, *, sb):
    c = pl.program_id(1)
    nb = sb * H_B
    n = sb * CHUNK

    @pl.when(c == 0)
    def _():
        st_scr[...] = jnp.swapaxes(s0_ref[...].reshape(nb, N_B, N_B), 1, 2)
        prev_scr[...] = sh0_ref[...]

    row = lax.broadcasted_iota(jnp.int32, (CHUNK, 1), 0)
    parts = []
    for s in range(sb):
        pb = pb_ref[s]
        prev = jnp.where(row == 0, prev_scr[s], pltpu.roll(pb, shift=1, axis=0))
        prev_scr[s] = pb[CHUNK - 1:, :]
        parts.append(pb + mu_ref[...] * (prev - pb))
    xs = jnp.concatenate(parts, axis=0)
    r = xs[:, :D_B]
    k = xs[:, D_B:2 * D_B]
    v = xs[:, 2 * D_B:3 * D_B]
    o = 3 * D_B
    xw, xa, xg = xs[:, o:o + W_RANK], xs[:, o + W_RANK:o + W_RANK + A_RANK], xs[:, o + W_RANK + A_RANK:]
    z = w0_ref[...] + _dot(jnp.tanh(xw).astype(BF16), wup_ref[...])
    logw = -math.exp(-0.5) * jax.nn.sigmoid(z)
    gate = jax.nn.sigmoid(a0_ref[...] + _dot(xa.astype(BF16), aup_ref[...]))
    g = _dot(jax.nn.sigmoid(xg).astype(BF16), gup_ref[...])
    kkf = k * kk_ref[...]
    k2 = k * (1.0 + (gate - 1.0) * ka_ref[...])
    ri = lax.broadcasted_iota(jnp.int32, (n, 1), 0)
    ci = lax.broadcasted_iota(jnp.int32, (1, n), 1)
    same_stream = jnp.right_shift(ri, 6) == jnp.right_shift(ci, 6)
    ones_tri = jnp.where(ri >= ci, jnp.where(same_stream, 1.0, 0.0), 0.0).astype(BF16)
    cs = sum(_dot(ones_tri, piece) for piece in _split3(logw))
    p_in = jnp.exp(cs)
    p_prev = jnp.exp(cs - logw)
    p_inv = jnp.exp(-cs)

    def heads3(x):
        return jnp.stack([x[s * CHUNK:(s + 1) * CHUNK, h * N_B:(h + 1) * N_B]
                          for s in range(sb) for h in range(H_B)], axis=0)

    def per_head(ref):
        return jnp.stack([ref[:, h * N_B:(h + 1) * N_B] for _ in range(sb) for h in range(H_B)], axis=0)

    r3, k3, v3 = heads3(r), heads3(k2), heads3(v)
    kkf3, kg3 = heads3(kkf), heads3(kkf * gate)
    pin3, pprev3, pinv3 = heads3(p_in), heads3(p_prev), heads3(p_inv)
    rs = lax.rsqrt(jnp.sum(kkf3 * kkf3, axis=-1, keepdims=True) + 1e-12)
    pend3 = pin3[:, CHUNK - 1:, :]
    at32 = -(kkf3 * rs) * pprev3
    kt32 = k3 * pinv3
    bt32 = kg3 * rs * pinv3
    rt32 = r3 * pin3
    at, rt = at32.astype(BF16), rt32.astype(BF16)
    bk = jnp.concatenate([bt32, kt32], axis=1).astype(BF16)
    bhkh_t = jnp.swapaxes(jnp.concatenate([bt32 * pend3, kt32 * pend3], axis=1), 1, 2).astype(BF16)
    vb = v3.astype(BF16)

    tt = lax.broadcasted_iota(jnp.int32, (1, CHUNK, 2 * CHUNK), 1)
    ss = lax.broadcasted_iota(jnp.int32, (1, CHUNK, 2 * CHUNK), 2) & (CHUNK - 1)
    a_top = jnp.where(tt > ss, _bmm_nt(at, bk), 0.0)
    a_bot = jnp.where(tt >= ss, _bmm_nt(rt, bk), 0.0).astype(BF16)
    zero_v = jnp.concatenate([jnp.zeros((nb, CHUNK, N_B), BF16), vb], axis=1)
    zz = jnp.concatenate([at32, _bmm(a_top.astype(BF16), zero_v)], axis=-1)
    x = a_top[:, :, :CHUNK]
    for lvl in range(6):
        xb = x.astype(BF16)
        if lvl < 5:
            res = _bmm(xb, jnp.concatenate([zz.astype(BF16), xb], axis=-1))
            zz = zz + res[:, :, :2 * N_B]
            x = res[:, :, 2 * N_B:]
        else:
            zz = zz + _bmm(xb, zz.astype(BF16))
    w_m, u0 = zz[:, :, :N_B].astype(BF16), zz[:, :, N_B:].astype(BF16)
    res2 = _bmm(jnp.concatenate([a_bot[:, :, :CHUNK], bhkh_t[:, :, :CHUNK]], axis=1), w_m)
    q_hat = rt32 + res2[:, :CHUNK]
    eye = lax.broadcasted_iota(jnp.int32, (1, N_B, N_B), 1) == lax.broadcasted_iota(jnp.int32, (1, N_B, N_B), 2)
    m_t = jnp.where(eye, pend3, 0.0) + res2[:, CHUNK:]
    lhs = jnp.concatenate([jnp.concatenate([a_bot, q_hat.astype(BF16)], axis=-1),
                           jnp.concatenate([bhkh_t, m_t.astype(BF16)], axis=-1)], axis=1)
    rhs = jnp.concatenate([u0, vb, st_scr[...].astype(BF16)], axis=1)
    res3 = _bmm(lhs, rhs)
    y3 = res3[:, :CHUNK]
    st_scr[...] = res3[:, CHUNK:]

    mean = jnp.mean(y3, axis=-1, keepdims=True)
    var = jnp.mean(jnp.square(y3 - mean), axis=-1, keepdims=True)
    y3 = (y3 - mean) * lax.rsqrt(var + GN_EPS) * per_head(gng_ref) + per_head(gnb_ref)
    y3 = y3 + jnp.sum(r3 * k3 * per_head(rk_ref), axis=-1, keepdims=True) * v3
    for s in range(sb):
        y2 = jnp.concatenate([y3[s * H_B + h] for h in range(H_B)], axis=-1)
        y_ref[s] = (y2 * g[s * CHUNK:(s + 1) * CHUNK]).astype(y_ref.dtype)

    @pl.when(c == pl.num_programs(1) - 1)
    def _():
        so_ref[...] = jnp.swapaxes(st_scr[...], 1, 2).reshape(sb, H_B, N_B, N_B)


def wkv_group(pb, l, ls, shift0, s0, mu, w0, w_up, a0, a_up, g_up, k_k, k_a, r_k, gn_g, gn_b):
    b, t, _ = pb.shape
    sb = min(WKV_STREAMS, b)
    small = [mu, w0, w_up, a0, a_up, g_up, k_k, k_a, r_k, gn_g, gn_b]
    return pl.pallas_call(
        functools.partial(_wkv_kernel, sb=sb),
        grid=(b // sb, t // CHUNK),
        in_specs=[pl.BlockSpec((sb, CHUNK, D_SHIFT), lambda bi, c: (bi, c, 0)),
                  pl.BlockSpec((None, sb, 1, D_SHIFT), lambda bi, c: (ls, bi, 0, 0)),
                  pl.BlockSpec((None, sb, H_B, N_B, N_B), lambda bi, c: (ls, bi, 0, 0, 0))]
        + [_layer_spec(a, l) for a in small],
        out_specs=[pl.BlockSpec((sb, CHUNK, D_B), lambda bi, c: (bi, c, 0)),
                   pl.BlockSpec((sb, H_B, N_B, N_B), lambda bi, c: (bi, 0, 0, 0))],
        out_shape=[jax.ShapeDtypeStruct((b, t, D_B), BF16), jax.ShapeDtypeStruct((b, H_B, N_B, N_B), F32)],
        scratch_shapes=[pltpu.VMEM((sb * H_B, N_B, N_B), F32), pltpu.VMEM((sb, 1, D_SHIFT), F32)],
        compiler_params=_cparams("parallel", "arbitrary"),
        name="wkv_group",
    )(pb, shift0, s0, *small)


def _out_kernel(x_ref, ya_ref, yb_ref, wo_ref, g_ref, wq_ref, mk_ref, mv_ref, wmo_ref, o_ref, *, spt):
    x = x_ref[...] + _dot(ya_ref[...], wo_ref[:H_A * DV, :]) + _dot(yb_ref[...], wo_ref[H_A * DV:, :])
    q = _dot(_rms(x, g_ref[...]).astype(BF16), wq_ref[...])
    ts = x.shape[0] // spt
    rows = []
    for s in range(spt):
        heads = []
        for hd in range(MEM_HEADS):
            halves = [pl.ds(j * MEM_HEADS + hd, N_MEM, stride=MEM_ROWS) for j in range(MEM_LANE_HALVES)]
            qh = q[s * ts:(s + 1) * ts, hd * MEM_HD:(hd + 1) * MEM_HD].astype(BF16)
            sc = sum(_dot_nt(qh[:, j * 128:(j + 1) * 128], mk_ref[s, rws, :].astype(BF16))
                     for j, rws in enumerate(halves)) / math.sqrt(MEM_HD)
            sc = jnp.exp(sc - jnp.max(sc, axis=-1, keepdims=True))
            p = (sc / jnp.sum(sc, axis=-1, keepdims=True)).astype(BF16)
            heads += [_dot(p, mv_ref[s, rws, :].astype(BF16)) for rws in halves]
        rows.append(jnp.concatenate(heads, axis=-1))
    att = rows[0] if spt == 1 else jnp.concatenate(rows, axis=0)
    o_ref[...] = x + _dot(att.astype(BF16), wmo_ref[...])


def mixer_out_xattn(x, ya, yb, l, w_out, g, w_mq, lm, mk, mv, w_mo, *, t_stream):
    n, d = x.shape
    tm = min(TOKEN_TILE, n)
    spt = max(1, tm // t_stream)
    tps = max(1, t_stream // tm)
    tok = lambda w: pl.BlockSpec((tm, w), lambda i: (i, 0))
    mem = pl.BlockSpec((None, spt, N_MEM * MEM_ROWS, 128), lambda i: (lm, i // tps, 0, 0))
    return pl.pallas_call(
        functools.partial(_out_kernel, spt=spt),
        grid=(n // tm,),
        in_specs=[tok(d), tok(H_A * DV), tok(D_B), _layer_spec(w_out, l), _layer_spec(g, l),
                  _layer_spec(w_mq, l), mem, mem, _layer_spec(w_mo, l)],
        out_specs=tok(d),
        out_shape=jax.ShapeDtypeStruct((n, d), F32),
        compiler_params=_cparams("parallel"),
        name="mixer_out_xattn",
    )(x, ya, yb, w_out, g, w_mq, mk, mv, w_mo)


def _memkv_kernel(m_ref, g_ref, wk_ref, wv_ref, k_ref, v_ref):
    m = _rms(m_ref[...], g_ref[...]).astype(BF16)
    k_ref[...] = _dot(m, wk_ref[...])
    v_ref[...] = _dot(m, wv_ref[...])


def mem_kv(mem, l, g, w_mk, w_mv):
    n, d = mem.shape
    tm = min(TOKEN_TILE, n)
    tok = pl.BlockSpec((tm, d), lambda i: (i, 0))
    return pl.pallas_call(
        _memkv_kernel,
        grid=(n // tm,),
        in_specs=[tok, _layer_spec(g, l), _layer_spec(w_mk, l), _layer_spec(w_mv, l)],
        out_specs=[tok, tok],
        out_shape=[jax.ShapeDtypeStruct((n, d), F32)] * 2,
        compiler_params=_cparams("parallel"),
        name="mem_kv",
    )(mem, g, w_mk, w_mv)


def _rot_half_cols(w):
    half = DR // 2
    shp = w.shape
    w = w.reshape(shp[:-1] + (shp[-1] // DR, 2, half))
    return jnp.concatenate([-w[..., 1:, :], w[..., :1, :]], axis=-2).reshape(shp)


def _mem_rows(a):
    lead = a.shape[:-3]
    a = a.reshape(lead + (N_MEM, MEM_HEADS, MEM_LANE_HALVES, 128))
    return jnp.swapaxes(a, -2, -3).reshape(lead + (N_MEM * MEM_ROWS, 128))


def _rope_tables(pos, reps):
    half = DR // 2
    inv = ROPE_THETA ** (-jnp.arange(half, dtype=F32) / half)
    ang = pos.astype(F32)[:, None] * inv[None, :]
    cos = jnp.tile(jnp.cos(ang), (1, 2 * reps))
    sin = jnp.tile(jnp.sin(ang), (1, 2 * reps))
    return cos, sin


def _group_tables(pos, n_tokens):
    rows = max(pos.shape[0], min(TOKEN_TILE, n_tokens))
    pos = jnp.tile(pos, rows // pos.shape[0])
    return _rope_tables(pos, H_A) + _rope_tables(pos, 1)


def _layer_group(x, tabs, lw, l, b, t, past_ckv, past_kr, ls, wkv0, shift0, lm, mk, mv, *, tq, kb, final_norm):
    x = ffn_half(x, l, lw["ffn1_norm"], lw["ffn1_wg"], lw["ffn1_wu"], lw["ffn1_wd"], lw["final_norm"],
                 final_norm=False)
    q, ckv, kr, pb = mixer_proj(x, l, lw["mix_norm"], lw["w_in_ext"], lw["q_norm"], lw["w_uq_ext"], lw["w_uk"],
                                lw["kv_norm"], *tabs)
    ckv3, kr3 = ckv.reshape(b, t, KV_RANK), kr.reshape(b, t, DR)
    ya = mla_attention(q, ckv3, kr3, l, past_ckv, past_kr, lw["w_uv_t"], tq=tq, kb=kb)
    pb3 = pb.reshape(b, t, D_SHIFT)
    yb, wkv = wkv_group(pb3, l, ls, shift0, wkv0, *(lw[n] for n in (
        "shift_mu", "w0", "w_up", "a0", "a_up", "g_up", "k_k", "k_a", "r_k", "gn_gain", "gn_bias")))
    x = mixer_out_xattn(x, ya.reshape(b * t, H_A * DV), yb.reshape(b * t, D_B), l, lw["w_out"], lw["xattn_norm"],
                        lw["w_mq"], lm, mk, mv, lw["w_mo"], t_stream=t)
    x = ffn_half(x, l, lw["ffn2_norm"], lw["ffn2_wg"], lw["ffn2_wu"], lw["ffn2_wd"], lw["final_norm"],
                 final_norm=final_norm)
    return x, ckv3, kr3, wkv, pb3[:, -1:, :]


def kernel(x_prompt, x_sample, mem_prompt, cache_ckv, cache_krope, cache_mem_k, cache_mem_v, state_wkv, state_shift, ffn1_norm, ffn1_w_gate, ffn1_w_up, ffn1_w_down, mix_norm, w_in, q_norm, w_uq, kv_norm, w_uk, w_uv, shift_mu, w0, w_up, a0, a_up, g_up, k_k, k_a, r_k, gn_gain, gn_bias, w_out, xattn_norm, mem_kv_norm, w_mq, w_mk, w_mv, w_mo, ffn2_norm, ffn2_w_gate, ffn2_w_up, ffn2_w_down, final_norm):
    depth = w_in.shape[0]
    b_p, t_p, d = x_prompt.shape
    b_s, t_s, _ = x_sample.shape
    n_past = cache_ckv.shape[2]
    bf = lambda a: a.astype(BF16)
    vec = lambda a: a.reshape(depth, 1, -1)

    o = Q_RANK + KV_RANK
    w_kr = w_in[..., o:o + DR]
    w_in_ext = jnp.concatenate(
        [w_in[..., :o + DR], _rot_half_cols(w_kr), jnp.zeros(w_in.shape[:2] + (PROJ_HEAD - o - 2 * DR,), w_in.dtype),
         w_in[..., o + DR:]], axis=-1)
    uq = w_uq.reshape(depth, Q_RANK, H_A, DN + DR)
    uq_nope = uq[..., :DN].reshape(depth, Q_RANK, H_A * DN)
    uq_rope = uq[..., DN:].reshape(depth, Q_RANK, H_A * DR)
    w_uq_ext = jnp.concatenate([uq_nope, uq_rope, _rot_half_cols(uq_rope)], axis=-1)
    lw = dict(
        ffn1_norm=vec(ffn1_norm), ffn1_wg=bf(ffn1_w_gate), ffn1_wu=bf(ffn1_w_up), ffn1_wd=bf(ffn1_w_down),
        mix_norm=vec(mix_norm), w_in_ext=bf(w_in_ext), q_norm=vec(q_norm), w_uq_ext=bf(w_uq_ext),
        kv_norm=vec(kv_norm), w_uk=bf(w_uk), w_uv_t=bf(jnp.swapaxes(w_uv, -1, -2)), shift_mu=vec(shift_mu), w0=vec(w0), w_up=bf(w_up),
        a0=vec(a0), a_up=bf(a_up), g_up=bf(g_up), k_k=vec(k_k), k_a=vec(k_a), r_k=vec(r_k), gn_gain=vec(gn_gain),
        gn_bias=vec(gn_bias), w_out=bf(w_out), xattn_norm=vec(xattn_norm), w_mq=bf(w_mq), w_mo=bf(w_mo),
        ffn2_norm=vec(ffn2_norm), ffn2_wg=bf(ffn2_w_gate), ffn2_wu=bf(ffn2_w_up), ffn2_wd=bf(ffn2_w_down),
        final_norm=final_norm.reshape(1, 1, d))
    mem_norm, w_mk_b, w_mv_b = vec(mem_kv_norm), bf(w_mk), bf(w_mv)

    tabs_p = _group_tables(jnp.arange(t_p, dtype=jnp.int32), b_p * t_p)
    tabs_s = _group_tables(n_past + jnp.arange(t_s, dtype=jnp.int32), b_s * t_s)
    wkv_zero = jnp.zeros((1, b_p, H_B, N_B, N_B), x_prompt.dtype)
    shift_zero = jnp.zeros((1, b_p, 1, D_SHIFT), x_prompt.dtype)
    krope_t = jnp.swapaxes(cache_krope, -1, -2)
    mem_k_s, mem_v_s = _mem_rows(cache_mem_k), _mem_rows(cache_mem_v)
    kb_s = -(-(n_past + t_s) // (3 * 128)) * 128

    xp = x_prompt.reshape(b_p * t_p, d)
    xs = x_sample.reshape(b_s * t_s, d)
    mem2 = mem_prompt.reshape(b_p * N_MEM, d)
    outs_p = [[] for _ in range(6)]
    outs_s = [[] for _ in range(4)]
    for l in range(depth):
        last = l == depth - 1
        mk, mv = (a.reshape(b_p, N_MEM, MEM_HEADS, MEM_HD) for a in mem_kv(mem2, l, mem_norm, w_mk_b, w_mv_b))
        xp, c1, k1, s1, h1 = _layer_group(xp, tabs_p, lw, l, b_p, t_p, None, None, 0, wkv_zero, shift_zero, 0,
                                          _mem_rows(mk[None]), _mem_rows(mv[None]), tq=256, kb=512,
                                          final_norm=last)
        for acc, val in zip(outs_p, (c1, k1, mk, mv, s1, h1)):
            acc.append(val)
        xs, c2, k2, s2, h2 = _layer_group(xs, tabs_s, lw, l, b_s, t_s, cache_ckv, krope_t, l, state_wkv,
                                          state_shift, l, mem_k_s, mem_v_s, tq=t_s, kb=kb_s, final_norm=last)
        for acc, val in zip(outs_s, (c2, k2, s2, h2)):
            acc.append(val)
    stack = lambda seq: jnp.stack(seq)
    return (xp.reshape(b_p, t_p, d), xs.reshape(b_s, t_s, d),
            stack(outs_p[0]), stack(outs_p[1]), stack(outs_p[2]), stack(outs_p[3]), stack(outs_p[4]), stack(outs_p[5]),
            stack(outs_s[0]), stack(outs_s[1]), stack(outs_s[2]), stack(outs_s[3]))
```

```python
import functools
import math

import jax
import jax.numpy as jnp
from jax import lax
from jax.experimental import pallas as pl
from jax.experimental.pallas import tpu as pltpu

F32 = jnp.float32
BF16 = jnp.bfloat16

D_MODEL = 1024
CHUNK = 64
H_A, DN, DR, DV = 8, 64, 32, 64
Q_RANK, KV_RANK = 256, 128
DQK = KV_RANK + DR
ROPE_THETA = 10000.0
MLA_SCALE = 1.0 / math.sqrt(DN + DR)
LOG2E = math.log2(math.e)
H_B, N_B = 8, 64
D_B = H_B * N_B
W_RANK, A_RANK, G_RANK = 64, 64, 128
D_SHIFT = 3 * D_B + W_RANK + A_RANK + G_RANK
N_MEM, MEM_HEADS = 256, 4
MEM_HD = D_MODEL // MEM_HEADS
MEM_LANE_HALVES = MEM_HD // 128
MEM_ROWS = MEM_LANE_HALVES * MEM_HEADS
D_FF = 2816
EPS = 1e-6
GN_EPS = 64e-5
NEG_INF = -1e30

PROJ_HEAD = 512
D_PROJ = PROJ_HEAD + D_SHIFT

TOKEN_TILE = 512
WKV_STREAMS = 4
VMEM_LIMIT = 56 * 1024 * 1024


def _cparams(*sem):
    return pltpu.CompilerParams(dimension_semantics=sem, vmem_limit_bytes=VMEM_LIMIT)


def _layer_spec(arr, l):
    nd = arr.ndim - 1
    return pl.BlockSpec((None,) + arr.shape[1:], lambda *_: (l,) + (0,) * nd, pipeline_mode=pl.Buffered(1))


def _rms(x, g):
    return x * lax.rsqrt(jnp.mean(x * x, axis=-1, keepdims=True) + EPS) * g


def _dot(a, b):
    return jnp.dot(a, b, preferred_element_type=F32)


def _dot_nt(a, b):
    return lax.dot_general(a, b, (((1,), (1,)), ((), ())), preferred_element_type=F32)


def _bmm(a, b):
    return jnp.einsum("gmk,gkn->gmn", a, b, preferred_element_type=F32)


def _bmm_nt(a, b):
    return jnp.einsum("gmk,gnk->gmn", a, b, preferred_element_type=F32)


def _ffn_kernel(x_ref, g_ref, wg_ref, wu_ref, wd_ref, gf_ref, o_ref, *, final_norm):
    x = x_ref[...]
    h = _rms(x, g_ref[...]).astype(BF16)
    gate = _dot(h, wg_ref[...])
    up = _dot(h, wu_ref[...])
    act = (gate * jax.nn.sigmoid(gate) * up).astype(BF16)
    y = x + 0.5 * _dot(act, wd_ref[...])
    if final_norm:
        y = _rms(y, gf_ref[...])
    o_ref[...] = y


def ffn_half(x, l, g, wg, wu, wd, gf, *, final_norm):
    n, d = x.shape
    tm = min(TOKEN_TILE, n)
    return pl.pallas_call(
        functools.partial(_ffn_kernel, final_norm=final_norm),
        grid=(n // tm,),
        in_specs=[pl.BlockSpec((tm, d), lambda i: (i, 0)), _layer_spec(g, l), _layer_spec(wg, l),
                  _layer_spec(wu, l), _layer_spec(wd, l), _layer_spec(gf, 0)],
        out_specs=pl.BlockSpec((tm, d), lambda i: (i, 0)),
        out_shape=jax.ShapeDtypeStruct((n, d), F32),
        compiler_params=_cparams("parallel"),
        name="ffn_half",
    )(x, g, wg, wu, wd, gf)


def _proj_kernel(x_ref, g_ref, win_ref, qn_ref, wuq_ref, wuk_ref, kvn_ref, cq_ref, sq_ref, ck_ref, sk_ref,
                 q_ref, ckv_ref, kr_ref, pb_ref):
    h = _rms(x_ref[...], g_ref[...]).astype(BF16)
    proj = _dot(h, win_ref[...])
    c_q = proj[:, :Q_RANK]
    c_kv = proj[:, Q_RANK:Q_RANK + KV_RANK]
    o = Q_RANK + KV_RANK
    k_r, k_rot = proj[:, o:o + DR], proj[:, o + DR:o + 2 * DR]
    ckv_ref[...] = _rms(c_kv, kvn_ref[...])
    kr_ref[...] = k_r * ck_ref[...] + k_rot * sk_ref[...]
    pb_ref[...] = proj[:, PROJ_HEAD:]
    qf = _dot(_rms(c_q, qn_ref[...]).astype(BF16), wuq_ref[...])
    nq = H_A * DN
    q_rope = (qf[:, nq:nq + H_A * DR] * cq_ref[...] + qf[:, nq + H_A * DR:] * sq_ref[...]) * (MLA_SCALE * LOG2E)
    for hd in range(H_A):
        q_lat = _dot_nt(qf[:, hd * DN:(hd + 1) * DN].astype(BF16), wuk_ref[hd]) * (MLA_SCALE * LOG2E)
        q_ref[hd, :, :KV_RANK] = q_lat.astype(BF16)
        q_ref[hd, :, KV_RANK:] = q_rope[:, hd * DR:(hd + 1) * DR].astype(BF16)


def mixer_proj(x, l, g, w_in_ext, q_norm, w_uq_ext, w_uk, kv_norm, cos_q, sin_q, cos_k, sin_k):
    n, d = x.shape
    tm = min(TOKEN_TILE, n)
    n_tab = cos_q.shape[0] // tm
    tok = lambda w: pl.BlockSpec((tm, w), lambda i: (i, 0))
    tab = lambda w: pl.BlockSpec((tm, w), lambda i: (i % n_tab, 0))
    return pl.pallas_call(
        _proj_kernel,
        grid=(n // tm,),
        in_specs=[tok(d), _layer_spec(g, l), _layer_spec(w_in_ext, l), _layer_spec(q_norm, l),
                  _layer_spec(w_uq_ext, l), _layer_spec(w_uk, l), _layer_spec(kv_norm, l),
                  tab(H_A * DR), tab(H_A * DR), tab(DR), tab(DR)],
        out_specs=[pl.BlockSpec((H_A, tm, DQK), lambda i: (0, i, 0)), tok(KV_RANK), tok(DR), tok(D_SHIFT)],
        out_shape=[jax.ShapeDtypeStruct((H_A, n, DQK), BF16), jax.ShapeDtypeStruct((n, KV_RANK), F32),
                   jax.ShapeDtypeStruct((n, DR), F32), jax.ShapeDtypeStruct((n, D_SHIFT), F32)],
        compiler_params=_cparams("parallel"),
        name="mixer_proj",
    )(x, g, w_in_ext, q_norm, w_uq_ext, w_uk, kv_norm, cos_q, sin_q, cos_k, sin_k)


def _mla_kernel(*refs, tq, kb, n_past, t_new, has_past):
    if has_past:
        q_ref, ckv_ref, kr_ref, pckv_ref, pkr_ref, wuvt_ref, o_ref, kcat, vt = refs
    else:
        q_ref, ckv_ref, kr_ref, wuvt_ref, o_ref, kcat, vt = refs
        pckv_ref = pkr_ref = None
    i = pl.program_id(1)
    t_all = n_past + t_new
    t_pad = kcat.shape[0]

    @pl.when(i == 0)
    def _():
        if t_pad > t_all:
            kcat[t_all:, :] = jnp.zeros((t_pad - t_all, DQK), BF16)
            vt[:, t_all:] = jnp.zeros((KV_RANK, t_pad - t_all), BF16)
        pieces = [(0, pckv_ref, pkr_ref, n_past)] if has_past else []
        for base, c_ref, r_ref, n in pieces + [(n_past, ckv_ref, kr_ref, t_new)]:
            kcat[base:base + n, :KV_RANK] = c_ref[0].astype(BF16)
            step_t = min(n, 512)
            for o in range(0, n, step_t):
                vt[:, base + o:base + o + step_t] = c_ref[0, o:o + step_t, :].T.astype(BF16)
                if r_ref is pkr_ref:
                    kcat[base + o:base + o + step_t, KV_RANK:] = r_ref[0, :, o:o + step_t].T.astype(BF16)
            if r_ref is not pkr_ref:
                kcat[base:base + n, KV_RANK:] = r_ref[0].astype(BF16)

    rows = H_A * tq
    q = q_ref[...].reshape(rows, DQK)
    t_in = lax.broadcasted_iota(jnp.int32, (1, rows), 1) & (tq - 1)
    limit = ((n_past + i * tq + t_in) // CHUNK + 1) * CHUNK
    n_blocks = (n_past + (i + 1) * tq + kb - 1) // kb

    def scores(j):
        return _dot_nt(kcat[pl.ds(pl.multiple_of(j * kb, kb), kb), :], q)

    def update(j, carry, st, masked):
        m, l, acc = carry
        if masked:
            k_pos = j * kb + lax.broadcasted_iota(jnp.int32, (kb, 1), 0)
            st = jnp.where(k_pos < limit, st, NEG_INF)
        m_new = jnp.maximum(m, jnp.max(st, axis=0, keepdims=True))
        alpha = jnp.exp2(m - m_new)
        p = jnp.exp2(st - m_new)
        l = alpha * l + jnp.sum(p, axis=0, keepdims=True)
        acc = alpha * acc + _dot(vt[:, pl.ds(pl.multiple_of(j * kb, kb), kb)], p.astype(BF16))
        return m_new, l, acc

    def pair(jj, carry):
        st0, st1 = scores(2 * jj), scores(2 * jj + 1)
        return update(2 * jj + 1, update(2 * jj, carry, st0, False), st1, False)

    def single(j, carry):
        return update(j, carry, scores(j), False)

    m0 = jnp.full((1, rows), NEG_INF, F32)
    l0 = jnp.zeros((1, rows), F32)
    a0 = jnp.zeros((KV_RANK, rows), F32)
    n_pairs = (n_blocks - 1) // 2
    carry = lax.fori_loop(0, n_pairs, pair, (m0, l0, a0))
    carry = lax.fori_loop(2 * n_pairs, n_blocks - 1, single, carry)
    _, l, acc = update(n_blocks - 1, carry, scores(n_blocks - 1), True)
    o_lat = (acc / l).astype(BF16)
    outs = [_dot(wuvt_ref[hd], o_lat[:, hd * tq:(hd + 1) * tq]) for hd in range(H_A)]
    o_ref[0] = jnp.concatenate(outs, axis=0).T.astype(o_ref.dtype)


def mla_attention(q, ckv, kr, l, past_ckv, past_kr, w_uv_t, *, tq, kb):
    b, t, _ = ckv.shape
    has_past = past_ckv is not None
    n_past = past_ckv.shape[2] if has_past else 0
    t_pad = -(-(n_past + t) // kb) * kb
    nq = t // tq
    for i in range(nq):
        first_limit = ((n_past + i * tq) // CHUNK + 1) * CHUNK
        assert first_limit // kb >= -(-(n_past + (i + 1) * tq) // kb) - 1, (tq, kb, n_past, i)
    stream = lambda n, w: pl.BlockSpec((1, n, w), lambda bi, i: (bi, 0, 0))
    past_c = pl.BlockSpec((None, 1, n_past, KV_RANK), lambda bi, i: (l, bi, 0, 0))
    past_r = pl.BlockSpec((None, 1, DR, n_past), lambda bi, i: (l, bi, 0, 0))
    in_specs = [pl.BlockSpec((H_A, tq, DQK), lambda bi, i: (0, bi * nq + i, 0)), stream(t, KV_RANK), stream(t, DR)]
    args = [q, ckv, kr]
    if has_past:
        in_specs += [past_c, past_r]
        args += [past_ckv, past_kr]
    in_specs.append(_layer_spec(w_uv_t, l))
    args.append(w_uv_t)
    return pl.pallas_call(
        functools.partial(_mla_kernel, tq=tq, kb=kb, n_past=n_past, t_new=t, has_past=has_past),
        grid=(b, nq),
        in_specs=in_specs,
        out_specs=pl.BlockSpec((1, tq, H_A * DV), lambda bi, i: (bi, i, 0)),
        out_shape=jax.ShapeDtypeStruct((b, t, H_A * DV), BF16),
        scratch_shapes=[pltpu.VMEM((t_pad, DQK), BF16), pltpu.VMEM((KV_RANK, t_pad), BF16)],
        compiler_params=_cparams("parallel", "arbitrary"),
        name="mla_attention",
    )(*args)


def _split3(x):
    hi = x.astype(BF16)
    r1 = x - hi.astype(F32)
    mid = r1.astype(BF16)
    lo = (r1 - mid.astype(F32)).astype(BF16)
    return hi, mid, lo


def _wkv_kernel(pb_ref, sh0_ref, s0_ref, mu_ref, w0_ref, wup_ref, a0_ref, aup_ref, gup_ref, kk_ref, ka_ref,
                rk_ref, gng_ref, gnb_ref, seg_ref, y_ref, so_ref, st_scr, prev_scr, *, sb):
    c = pl.program_id(1)
    nb = sb * H_B
    n = sb * CHUNK

    @pl.when(c == 0)
    def _():
        st_scr[...] = jnp.swapaxes(s0_ref[...].reshape(nb, N_B, N_B), 1, 2)
        prev_scr[...] = sh0_ref[...]

    row = lax.broadcasted_iota(jnp.int32, (CHUNK, 1), 0)
    parts = []
    for s in range(sb):
        pb = pb_ref[s]
        prev = jnp.where(row == 0, prev_scr[s], pltpu.roll(pb, shift=1, axis=0))
        prev_scr[s] = pb[CHUNK - 1:, :]
        parts.append(pb + mu_ref[...] * (prev - pb))
    xs = jnp.concatenate(parts, axis=0)
    r = xs[:, :D_B]
    k = xs[:, D_B:2 * D_B]
    v = xs[:, 2 * D_B:3 * D_B]
    o = 3 * D_B
    xw, xa, xg = xs[:, o:o + W_RANK], xs[:, o + W_RANK:o + W_RANK + A_RANK], xs[:, o + W_RANK + A_RANK:]
    z = w0_ref[...] + _dot(jnp.tanh(xw).astype(BF16), wup_ref[...])
    logw = -math.exp(-0.5) * jax.nn.sigmoid(z)
    gate = jax.nn.sigmoid(a0_ref[...] + _dot(xa.astype(BF16), aup_ref[...]))
    g = _dot(jax.nn.sigmoid(xg).astype(BF16), gup_ref[...])
    kkf = k * kk_ref[...]
    k2 = k * (1.0 + (gate - 1.0) * ka_ref[...])
    ri = lax.broadcasted_iota(jnp.int32, (n, 1), 0)
    ci = lax.broadcasted_iota(jnp.int32, (1, n), 1)
    same_stream = jnp.right_shift(ri, 6) == jnp.right_shift(ci, 6)
    ones_tri = jnp.where(ri >= ci, jnp.where(same_stream, 1.0, 0.0), 0.0).astype(BF16)
    cs = sum(_dot(ones_tri, piece) for piece in _split3(logw))
    p_in = jnp.exp(cs)
    p_prev = jnp.exp(cs - logw)
    p_inv = jnp.exp(-cs)

    ones_blk = seg_ref[...]

    def seg_sum(x):
        hi = x.astype(BF16)
        lo = (x - hi.astype(F32)).astype(BF16)
        return _dot(hi, ones_blk) + _dot(lo, ones_blk)

    def heads3(x):
        return jnp.stack([x[s * CHUNK:(s + 1) * CHUNK, h * N_B:(h + 1) * N_B]
                          for s in range(sb) for h in range(H_B)], axis=0)

    kk = kkf * lax.rsqrt(seg_sum(kkf * kkf) + 1e-12)
    bonus = seg_sum(r * k2 * rk_ref[...]) * v
    at32, rt32 = heads3(-kk * p_prev), heads3(r * p_in)
    kt32, bt32 = heads3(k2 * p_inv), heads3(kk * gate * p_inv)
    v3 = heads3(v)
    pend3 = jnp.stack([p_in[(s + 1) * CHUNK - 1:(s + 1) * CHUNK, h * N_B:(h + 1) * N_B]
                       for s in range(sb) for h in range(H_B)], axis=0)
    at, rt = at32.astype(BF16), rt32.astype(BF16)
    bk = jnp.concatenate([bt32, kt32], axis=1).astype(BF16)
    bhkh_t = jnp.swapaxes(jnp.concatenate([bt32 * pend3, kt32 * pend3], axis=1), 1, 2).astype(BF16)
    vb = v3.astype(BF16)

    tt = lax.broadcasted_iota(jnp.int32, (1, CHUNK, 2 * CHUNK), 1)
    ss = lax.broadcasted_iota(jnp.int32, (1, CHUNK, 2 * CHUNK), 2) & (CHUNK - 1)
    a_top = jnp.where(tt > ss, _bmm_nt(at, bk), 0.0)
    a_bot = jnp.where(tt >= ss, _bmm_nt(rt, bk), 0.0).astype(BF16)
    zero_v = jnp.concatenate([jnp.zeros((nb, CHUNK, N_B), BF16), vb], axis=1)
    zz = jnp.concatenate([at32, _bmm(a_top.astype(BF16), zero_v)], axis=-1)
    x = a_top[:, :, :CHUNK]
    for lvl in range(6):
        xb = x.astype(BF16)
        if lvl < 5:
            res = _bmm(xb, jnp.concatenate([zz.astype(BF16), xb], axis=-1))
            zz = zz + res[:, :, :2 * N_B]
            x = res[:, :, 2 * N_B:]
        else:
            zz = zz + _bmm(xb, zz.astype(BF16))
    w_m, u0 = zz[:, :, :N_B].astype(BF16), zz[:, :, N_B:].astype(BF16)
    res2 = _bmm(jnp.concatenate([a_bot[:, :, :CHUNK], bhkh_t[:, :, :CHUNK]], axis=1), w_m)
    q_hat = rt32 + res2[:, :CHUNK]
    eye = lax.broadcasted_iota(jnp.int32, (1, N_B, N_B), 1) == lax.broadcasted_iota(jnp.int32, (1, N_B, N_B), 2)
    m_t = jnp.where(eye, pend3, 0.0) + res2[:, CHUNK:]
    lhs = jnp.concatenate([jnp.concatenate([a_bot, q_hat.astype(BF16)], axis=-1),
                           jnp.concatenate([bhkh_t, m_t.astype(BF16)], axis=-1)], axis=1)
    rhs = jnp.concatenate([u0, vb, st_scr[...].astype(BF16)], axis=1)
    res3 = _bmm(lhs, rhs)
    y3 = res3[:, :CHUNK]
    st_scr[...] = res3[:, CHUNK:]

    y = jnp.concatenate([jnp.concatenate([y3[s * H_B + h] for h in range(H_B)], axis=-1) for s in range(sb)], axis=0)
    d = y - seg_sum(y) * (1.0 / N_B)
    var = seg_sum(d * d) * (1.0 / N_B)
    y = (d * lax.rsqrt(var + GN_EPS) * gng_ref[...] + gnb_ref[...] + bonus) * g
    for s in range(sb):
        y_ref[s] = y[s * CHUNK:(s + 1) * CHUNK].astype(y_ref.dtype)

    @pl.when(c == pl.num_programs(1) - 1)
    def _():
        so_ref[...] = jnp.swapaxes(st_scr[...], 1, 2).reshape(sb, H_B, N_B, N_B)


def wkv_group(pb, l, ls, shift0, s0, mu, w0, w_up, a0, a_up, g_up, k_k, k_a, r_k, gn_g, gn_b):
    b, t, _ = pb.shape
    sb = min(WKV_STREAMS, b)
    small = [mu, w0, w_up, a0, a_up, g_up, k_k, k_a, r_k, gn_g, gn_b]
    head_of = jnp.arange(D_B, dtype=jnp.int32) // N_B
    same_head = (head_of[:, None] == head_of[None, :]).astype(BF16)
    return pl.pallas_call(
        functools.partial(_wkv_kernel, sb=sb),
        grid=(b // sb, t // CHUNK),
        in_specs=[pl.BlockSpec((sb, CHUNK, D_SHIFT), lambda bi, c: (bi, c, 0)),
                  pl.BlockSpec((None, sb, 1, D_SHIFT), lambda bi, c: (ls, bi, 0, 0)),
                  pl.BlockSpec((None, sb, H_B, N_B, N_B), lambda bi, c: (ls, bi, 0, 0, 0))]
        + [_layer_spec(a, l) for a in small]
        + [pl.BlockSpec((D_B, D_B), lambda bi, c: (0, 0), pipeline_mode=pl.Buffered(1))],
        out_specs=[pl.BlockSpec((sb, CHUNK, D_B), lambda bi, c: (bi, c, 0)),
                   pl.BlockSpec((sb, H_B, N_B, N_B), lambda bi, c: (bi, 0, 0, 0))],
        out_shape=[jax.ShapeDtypeStruct((b, t, D_B), BF16), jax.ShapeDtypeStruct((b, H_B, N_B, N_B), F32)],
        scratch_shapes=[pltpu.VMEM((sb * H_B, N_B, N_B), F32), pltpu.VMEM((sb, 1, D_SHIFT), F32)],
        compiler_params=_cparams("parallel", "arbitrary"),
        name="wkv_group",
    )(pb, shift0, s0, *small, same_head)


def _out_kernel(x_ref, ya_ref, yb_ref, wo_ref, g_ref, wq_ref, mk_ref, mv_ref, wmo_ref, o_ref, *, spt):
    x = x_ref[...] + _dot(ya_ref[...], wo_ref[:H_A * DV, :]) + _dot(yb_ref[...], wo_ref[H_A * DV:, :])
    q = _dot(_rms(x, g_ref[...]).astype(BF16), wq_ref[...])
    ts = x.shape[0] // spt
    halves = lambda hd: [pl.ds(j * MEM_HEADS + hd, N_MEM, stride=MEM_ROWS) for j in range(MEM_LANE_HALVES)]
    pairs = [(s, hd) for s in range(spt) for hd in range(MEM_HEADS)]
    scores = []
    for s, hd in pairs:
        qh = q[s * ts:(s + 1) * ts, hd * MEM_HD:(hd + 1) * MEM_HD].astype(BF16)
        scores.append(sum(_dot_nt(qh[:, j * 128:(j + 1) * 128], mk_ref[s, rws, :].astype(BF16))
                          for j, rws in enumerate(halves(hd))) / math.sqrt(MEM_HD))
    probs = []
    for sc in scores:
        sc = jnp.exp(sc - jnp.max(sc, axis=-1, keepdims=True))
        probs.append((sc / jnp.sum(sc, axis=-1, keepdims=True)).astype(BF16))
    outs = {}
    for (s, hd), p in zip(pairs, probs):
        outs[s, hd] = [_dot(p, mv_ref[s, rws, :].astype(BF16)) for rws in halves(hd)]
    rows = []
    for s in range(spt):
        rows.append(jnp.concatenate([o for hd in range(MEM_HEADS) for o in outs[s, hd]], axis=-1))
    att = rows[0] if spt == 1 else jnp.concatenate(rows, axis=0)
    o_ref[...] = x + _dot(att.astype(BF16), wmo_ref[...])


def mixer_out_xattn(x, ya, yb, l, w_out, g, w_mq, lm, mk, mv, w_mo, *, t_stream):
    n, d = x.shape
    tm = min(TOKEN_TILE, n)
    spt = max(1, tm // t_stream)
    tps = max(1, t_stream // tm)
    tok = lambda w: pl.BlockSpec((tm, w), lambda i: (i, 0))
    mem = pl.BlockSpec((None, spt, N_MEM * MEM_ROWS, 128), lambda i: (lm, i // tps, 0, 0))
    return pl.pallas_call(
        functools.partial(_out_kernel, spt=spt),
        grid=(n // tm,),
        in_specs=[tok(d), tok(H_A * DV), tok(D_B), _layer_spec(w_out, l), _layer_spec(g, l),
                  _layer_spec(w_mq, l), mem, mem, _layer_spec(w_mo, l)],
        out_specs=tok(d),
        out_shape=jax.ShapeDtypeStruct((n, d), F32),
        compiler_params=_cparams("parallel"),
        name="mixer_out_xattn",
    )(x, ya, yb, w_out, g, w_mq, mk, mv, w_mo)


def _memkv_kernel(m_ref, g_ref, wk_ref, wv_ref, k_ref, v_ref):
    m = _rms(m_ref[...], g_ref[...]).astype(BF16)
    k_ref[...] = _dot(m, wk_ref[...])
    v_ref[...] = _dot(m, wv_ref[...])


def mem_kv(mem, l, g, w_mk, w_mv):
    n, d = mem.shape
    tm = min(TOKEN_TILE, n)
    tok = pl.BlockSpec((tm, d), lambda i: (i, 0))
    return pl.pallas_call(
        _memkv_kernel,
        grid=(n // tm,),
        in_specs=[tok, _layer_spec(g, l), _layer_spec(w_mk, l), _layer_spec(w_mv, l)],
        out_specs=[tok, tok],
        out_shape=[jax.ShapeDtypeStruct((n, d), F32)] * 2,
        compiler_params=_cparams("parallel"),
        name="mem_kv",
    )(mem, g, w_mk, w_mv)


def _rot_half_cols(w):
    half = DR // 2
    shp = w.shape
    w = w.reshape(shp[:-1] + (shp[-1] // DR, 2, half))
    return jnp.concatenate([-w[..., 1:, :], w[..., :1, :]], axis=-2).reshape(shp)


def _mem_rows(a):
    lead = a.shape[:-3]
    a = a.reshape(lead + (N_MEM, MEM_HEADS, MEM_LANE_HALVES, 128))
    return jnp.swapaxes(a, -2, -3).reshape(lead + (N_MEM * MEM_ROWS, 128))


def _rope_tables(pos, reps):
    half = DR // 2
    inv = ROPE_THETA ** (-jnp.arange(half, dtype=F32) / half)
    ang = pos.astype(F32)[:, None] * inv[None, :]
    cos = jnp.tile(jnp.cos(ang), (1, 2 * reps))
    sin = jnp.tile(jnp.sin(ang), (1, 2 * reps))
    return cos, sin


def _group_tables(pos, n_tokens):
    rows = max(pos.shape[0], min(TOKEN_TILE, n_tokens))
    pos = jnp.tile(pos, rows // pos.shape[0])
    return _rope_tables(pos, H_A) + _rope_tables(pos, 1)


def _layer_group(x, tabs, lw, l, b, t, past_ckv, past_kr, ls, wkv0, shift0, lm, mk, mv, *, tq, kb, final_norm):
    x = ffn_half(x, l, lw["ffn1_norm"], lw["ffn1_wg"], lw["ffn1_wu"], lw["ffn1_wd"], lw["final_norm"],
                 final_norm=False)
    q, ckv, kr, pb = mixer_proj(x, l, lw["mix_norm"], lw["w_in_ext"], lw["q_norm"], lw["w_uq_ext"], lw["w_uk"],
                                lw["kv_norm"], *tabs)
    ckv3, kr3 = ckv.reshape(b, t, KV_RANK), kr.reshape(b, t, DR)
    ya = mla_attention(q, ckv3, kr3, l, past_ckv, past_kr, lw["w_uv_t"], tq=tq, kb=kb)
    pb3 = pb.reshape(b, t, D_SHIFT)
    yb, wkv = wkv_group(pb3, l, ls, shift0, wkv0, *(lw[n] for n in (
        "shift_mu", "w0", "w_up", "a0", "a_up", "g_up", "k_k", "k_a", "r_k", "gn_gain", "gn_bias")))
    x = mixer_out_xattn(x, ya.reshape(b * t, H_A * DV), yb.reshape(b * t, D_B), l, lw["w_out"], lw["xattn_norm"],
                        lw["w_mq"], lm, mk, mv, lw["w_mo"], t_stream=t)
    x = ffn_half(x, l, lw["ffn2_norm"], lw["ffn2_wg"], lw["ffn2_wu"], lw["ffn2_wd"], lw["final_norm"],
                 final_norm=final_norm)
    return x, ckv3, kr3, wkv, pb3[:, -1:, :]


def kernel(x_prompt, x_sample, mem_prompt, cache_ckv, cache_krope, cache_mem_k, cache_mem_v, state_wkv, state_shift, ffn1_norm, ffn1_w_gate, ffn1_w_up, ffn1_w_down, mix_norm, w_in, q_norm, w_uq, kv_norm, w_uk, w_uv, shift_mu, w0, w_up, a0, a_up, g_up, k_k, k_a, r_k, gn_gain, gn_bias, w_out, xattn_norm, mem_kv_norm, w_mq, w_mk, w_mv, w_mo, ffn2_norm, ffn2_w_gate, ffn2_w_up, ffn2_w_down, final_norm):
    depth = w_in.shape[0]
    b_p, t_p, d = x_prompt.shape
    b_s, t_s, _ = x_sample.shape
    n_past = cache_ckv.shape[2]
    bf = lambda a: a.astype(BF16)
    vec = lambda a: a.reshape(depth, 1, -1)

    o = Q_RANK + KV_RANK
    w_kr = w_in[..., o:o + DR]
    w_in_ext = jnp.concatenate(
        [w_in[..., :o + DR], _rot_half_cols(w_kr), jnp.zeros(w_in.shape[:2] + (PROJ_HEAD - o - 2 * DR,), w_in.dtype),
         w_in[..., o + DR:]], axis=-1)
    uq = w_uq.reshape(depth, Q_RANK, H_A, DN + DR)
    uq_nope = uq[..., :DN].reshape(depth, Q_RANK, H_A * DN)
    uq_rope = uq[..., DN:].reshape(depth, Q_RANK, H_A * DR)
    w_uq_ext = jnp.concatenate([uq_nope, uq_rope, _rot_half_cols(uq_rope)], axis=-1)
    lw = dict(
        ffn1_norm=vec(ffn1_norm), ffn1_wg=bf(ffn1_w_gate), ffn1_wu=bf(ffn1_w_up), ffn1_wd=bf(ffn1_w_down),
        mix_norm=vec(mix_norm), w_in_ext=bf(w_in_ext), q_norm=vec(q_norm), w_uq_ext=bf(w_uq_ext),
        kv_norm=vec(kv_norm), w_uk=bf(w_uk), w_uv_t=bf(jnp.swapaxes(w_uv, -1, -2)), shift_mu=vec(shift_mu),
        w0=vec(w0), w_up=bf(w_up), a0=vec(a0), a_up=bf(a_up), g_up=bf(g_up), k_k=vec(k_k), k_a=vec(k_a),
        r_k=vec(r_k), gn_gain=vec(gn_gain), gn_bias=vec(gn_bias), w_out=bf(w_out), xattn_norm=vec(xattn_norm),
        w_mq=bf(w_mq), w_mo=bf(w_mo), ffn2_norm=vec(ffn2_norm), ffn2_wg=bf(ffn2_w_gate), ffn2_wu=bf(ffn2_w_up),
        ffn2_wd=bf(ffn2_w_down), final_norm=final_norm.reshape(1, 1, d))
    mem_norm, w_mk_b, w_mv_b = vec(mem_kv_norm), bf(w_mk), bf(w_mv)

    tabs_p = _group_tables(jnp.arange(t_p, dtype=jnp.int32), b_p * t_p)
    tabs_s = _group_tables(n_past + jnp.arange(t_s, dtype=jnp.int32), b_s * t_s)
    wkv_zero = jnp.zeros((1, b_p, H_B, N_B, N_B), x_prompt.dtype)
    shift_zero = jnp.zeros((1, b_p, 1, D_SHIFT), x_prompt.dtype)
    krope_t = jnp.swapaxes(cache_krope, -1, -2)
    mem_k_s, mem_v_s = _mem_rows(cache_mem_k), _mem_rows(cache_mem_v)
    kb_s = -(-(n_past + t_s) // (3 * 128)) * 128

    xp = x_prompt.reshape(b_p * t_p, d)
    xs = x_sample.reshape(b_s * t_s, d)
    mem2 = mem_prompt.reshape(b_p * N_MEM, d)
    outs_p = [[] for _ in range(6)]
    outs_s = [[] for _ in range(4)]
    for l in range(depth):
        last = l == depth - 1
        mk, mv = (a.reshape(b_p, N_MEM, MEM_HEADS, MEM_HD) for a in mem_kv(mem2, l, mem_norm, w_mk_b, w_mv_b))
        xp, c1, k1, s1, h1 = _layer_group(xp, tabs_p, lw, l, b_p, t_p, None, None, 0, wkv_zero, shift_zero, 0,
                                          _mem_rows(mk[None]), _mem_rows(mv[None]), tq=256, kb=512,
                                          final_norm=last)
        for acc, val in zip(outs_p, (c1, k1, mk, mv, s1, h1)):
            acc.append(val)
        xs, c2, k2, s2, h2 = _layer_group(xs, tabs_s, lw, l, b_s, t_s, cache_ckv, krope_t, l, state_wkv,
                                          state_shift, l, mem_k_s, mem_v_s, tq=t_s, kb=kb_s, final_norm=last)
        for acc, val in zip(outs_s, (c2, k2, s2, h2)):
            acc.append(val)
    stack = lambda seq: jnp.stack(seq)
    return (xp.reshape(b_p, t_p, d), xs.reshape(b_s, t_s, d),
            stack(outs_p[0]), stack(outs_p[1]), stack(outs_p[2]), stack(outs_p[3]), stack(outs_p[4]), stack(outs_p[5]),
            stack(outs_s[0]), stack(outs_s[1]), stack(outs_s[2]), stack(outs_s[3]))
```

```python
import functools
import math

import jax
import jax.numpy as jnp
from jax import lax
from jax.experimental import pallas as pl
from jax.experimental.pallas import tpu as pltpu

F32 = jnp.float32
BF16 = jnp.bfloat16

D_MODEL = 1024
CHUNK = 64
H_A, DN, DR, DV = 8, 64, 32, 64
Q_RANK, KV_RANK = 256, 128
DQK = KV_RANK + DR
ROPE_THETA = 10000.0
MLA_SCALE = 1.0 / math.sqrt(DN + DR)
LOG2E = math.log2(math.e)
H_B, N_B = 8, 64
D_B = H_B * N_B
W_RANK, A_RANK, G_RANK = 64, 64, 128
D_SHIFT = 3 * D_B + W_RANK + A_RANK + G_RANK
N_MEM, MEM_HEADS = 256, 4
MEM_HD = D_MODEL // MEM_HEADS
MEM_LANE_HALVES = MEM_HD // 128
MEM_ROWS = MEM_LANE_HALVES * MEM_HEADS
D_FF = 2816
EPS = 1e-6
GN_EPS = 64e-5
NEG_INF = -1e30

PROJ_HEAD = 512
D_PROJ = PROJ_HEAD + D_SHIFT

TOKEN_TILE = 512
WKV_STREAMS = 4
VMEM_LIMIT = 56 * 1024 * 1024


def _cparams(*sem):
    return pltpu.CompilerParams(dimension_semantics=sem, vmem_limit_bytes=VMEM_LIMIT)


def _layer_spec(arr, l):
    nd = arr.ndim - 1
    return pl.BlockSpec((None,) + arr.shape[1:], lambda *_: (l,) + (0,) * nd, pipeline_mode=pl.Buffered(1))


def _rms(x, g):
    return x * lax.rsqrt(jnp.mean(x * x, axis=-1, keepdims=True) + EPS) * g


def _dot(a, b):
    return jnp.dot(a, b, preferred_element_type=F32)


def _dot_nt(a, b):
    return lax.dot_general(a, b, (((1,), (1,)), ((), ())), preferred_element_type=F32)


def _bmm(a, b):
    return jnp.einsum("gmk,gkn->gmn", a, b, preferred_element_type=F32)


def _bmm_nt(a, b):
    return jnp.einsum("gmk,gnk->gmn", a, b, preferred_element_type=F32)


def _ffn_kernel(x_ref, g_ref, wg_ref, wu_ref, wd_ref, gf_ref, o_ref, *, final_norm):
    x = x_ref[...]
    h = _rms(x, g_ref[...]).astype(BF16)
    gate = _dot(h, wg_ref[...])
    up = _dot(h, wu_ref[...])
    act = (gate * jax.nn.sigmoid(gate) * up).astype(BF16)
    y = x + 0.5 * _dot(act, wd_ref[...])
    if final_norm:
        y = _rms(y, gf_ref[...])
    o_ref[...] = y


def ffn_half(x, l, g, wg, wu, wd, gf, *, final_norm):
    n, d = x.shape
    tm = min(TOKEN_TILE, n)
    return pl.pallas_call(
        functools.partial(_ffn_kernel, final_norm=final_norm),
        grid=(n // tm,),
        in_specs=[pl.BlockSpec((tm, d), lambda i: (i, 0)), _layer_spec(g, l), _layer_spec(wg, l),
                  _layer_spec(wu, l), _layer_spec(wd, l), _layer_spec(gf, 0)],
        out_specs=pl.BlockSpec((tm, d), lambda i: (i, 0)),
        out_shape=jax.ShapeDtypeStruct((n, d), F32),
        compiler_params=_cparams("parallel"),
        name="ffn_half",
    )(x, g, wg, wu, wd, gf)


def _proj_kernel(x_ref, g_ref, win_ref, qn_ref, wuq_ref, wuk_ref, kvn_ref, cq_ref, sq_ref, ck_ref, sk_ref,
                 q_ref, ckv_ref, kr_ref, pb_ref):
    h = _rms(x_ref[...], g_ref[...]).astype(BF16)
    proj = _dot(h, win_ref[...])
    c_q = proj[:, :Q_RANK]
    c_kv = proj[:, Q_RANK:Q_RANK + KV_RANK]
    o = Q_RANK + KV_RANK
    k_r, k_rot = proj[:, o:o + DR], proj[:, o + DR:o + 2 * DR]
    ckv_ref[...] = _rms(c_kv, kvn_ref[...])
    kr_ref[...] = k_r * ck_ref[...] + k_rot * sk_ref[...]
    pb_ref[...] = proj[:, PROJ_HEAD:]
    qf = _dot(_rms(c_q, qn_ref[...]).astype(BF16), wuq_ref[...])
    nq = H_A * DN
    q_rope = (qf[:, nq:nq + H_A * DR] * cq_ref[...] + qf[:, nq + H_A * DR:] * sq_ref[...]) * (MLA_SCALE * LOG2E)
    for hd in range(H_A):
        q_lat = _dot_nt(qf[:, hd * DN:(hd + 1) * DN].astype(BF16), wuk_ref[hd]) * (MLA_SCALE * LOG2E)
        q_ref[hd, :, :KV_RANK] = q_lat.astype(BF16)
        q_ref[hd, :, KV_RANK:] = q_rope[:, hd * DR:(hd + 1) * DR].astype(BF16)


def mixer_proj(x, l, g, w_in_ext, q_norm, w_uq_ext, w_uk, kv_norm, cos_q, sin_q, cos_k, sin_k):
    n, d = x.shape
    tm = min(TOKEN_TILE, n)
    n_tab = cos_q.shape[0] // tm
    tok = lambda w: pl.BlockSpec((tm, w), lambda i: (i, 0))
    tab = lambda w: pl.BlockSpec((tm, w), lambda i: (i % n_tab, 0))
    return pl.pallas_call(
        _proj_kernel,
        grid=(n // tm,),
        in_specs=[tok(d), _layer_spec(g, l), _layer_spec(w_in_ext, l), _layer_spec(q_norm, l),
                  _layer_spec(w_uq_ext, l), _layer_spec(w_uk, l), _layer_spec(kv_norm, l),
                  tab(H_A * DR), tab(H_A * DR), tab(DR), tab(DR)],
        out_specs=[pl.BlockSpec((H_A, tm, DQK), lambda i: (0, i, 0)), tok(KV_RANK), tok(DR), tok(D_SHIFT)],
        out_shape=[jax.ShapeDtypeStruct((H_A, n, DQK), BF16), jax.ShapeDtypeStruct((n, KV_RANK), F32),
                   jax.ShapeDtypeStruct((n, DR), F32), jax.ShapeDtypeStruct((n, D_SHIFT), F32)],
        compiler_params=_cparams("parallel"),
        name="mixer_proj",
    )(x, g, w_in_ext, q_norm, w_uq_ext, w_uk, kv_norm, cos_q, sin_q, cos_k, sin_k)


def _mla_kernel(*refs, tq, kb, n_past, t_new, has_past):
    if has_past:
        q_ref, ckv_ref, kr_ref, pckv_ref, pkr_ref, wuvt_ref, o_ref, kcat, vt = refs
    else:
        q_ref, ckv_ref, kr_ref, wuvt_ref, o_ref, kcat, vt = refs
        pckv_ref = pkr_ref = None
    i = pl.program_id(1)
    t_all = n_past + t_new
    t_pad = kcat.shape[0]

    @pl.when(i == 0)
    def _():
        if t_pad > t_all:
            kcat[t_all:, :] = jnp.zeros((t_pad - t_all, DQK), BF16)
            vt[:, t_all:] = jnp.zeros((KV_RANK, t_pad - t_all), BF16)
        pieces = [(0, pckv_ref, pkr_ref, n_past)] if has_past else []
        for base, c_ref, r_ref, n in pieces + [(n_past, ckv_ref, kr_ref, t_new)]:
            kcat[base:base + n, :KV_RANK] = c_ref[0].astype(BF16)
            step_t = min(n, 512)
            for o in range(0, n, step_t):
                vt[:, base + o:base + o + step_t] = c_ref[0, o:o + step_t, :].T.astype(BF16)
                if r_ref is pkr_ref:
                    kcat[base + o:base + o + step_t, KV_RANK:] = r_ref[0, :, o:o + step_t].T.astype(BF16)
            if r_ref is not pkr_ref:
                kcat[base:base + n, KV_RANK:] = r_ref[0].astype(BF16)

    rows = H_A * tq
    q = q_ref[...].reshape(rows, DQK)
    t_in = lax.broadcasted_iota(jnp.int32, (1, rows), 1) & (tq - 1)
    limit = ((n_past + i * tq + t_in) // CHUNK + 1) * CHUNK
    n_blocks = (n_past + (i + 1) * tq + kb - 1) // kb

    def scores(j):
        return _dot_nt(kcat[pl.ds(pl.multiple_of(j * kb, kb), kb), :], q)

    def update(j, carry, st, masked):
        m, l, acc = carry
        if masked:
            k_pos = j * kb + lax.broadcasted_iota(jnp.int32, (kb, 1), 0)
            st = jnp.where(k_pos < limit, st, NEG_INF)
        m_new = jnp.maximum(m, jnp.max(st, axis=0, keepdims=True))
        alpha = jnp.exp2(m - m_new)
        p = jnp.exp2(st - m_new)
        l = alpha * l + jnp.sum(p, axis=0, keepdims=True)
        acc = alpha * acc + _dot(vt[:, pl.ds(pl.multiple_of(j * kb, kb), kb)], p.astype(BF16))
        return m_new, l, acc

    def pair(jj, carry):
        st0, st1 = scores(2 * jj), scores(2 * jj + 1)
        return update(2 * jj + 1, update(2 * jj, carry, st0, False), st1, False)

    def single(j, carry):
        return update(j, carry, scores(j), False)

    m0 = jnp.full((1, rows), NEG_INF, F32)
    l0 = jnp.zeros((1, rows), F32)
    a0 = jnp.zeros((KV_RANK, rows), F32)
    n_pairs = (n_blocks - 1) // 2
    carry = lax.fori_loop(0, n_pairs, pair, (m0, l0, a0))
    carry = lax.fori_loop(2 * n_pairs, n_blocks - 1, single, carry)
    _, l, acc = update(n_blocks - 1, carry, scores(n_blocks - 1), True)
    o_lat = (acc / l).astype(BF16)
    outs = [_dot(wuvt_ref[hd], o_lat[:, hd * tq:(hd + 1) * tq]) for hd in range(H_A)]
    o_ref[0] = jnp.concatenate(outs, axis=0).T.astype(o_ref.dtype)


def mla_attention(q, ckv, kr, l, past_ckv, past_kr, w_uv_t, *, tq, kb):
    b, t, _ = ckv.shape
    has_past = past_ckv is not None
    n_past = past_ckv.shape[2] if has_past else 0
    t_pad = -(-(n_past + t) // kb) * kb
    nq = t // tq
    for i in range(nq):
        first_limit = ((n_past + i * tq) // CHUNK + 1) * CHUNK
        assert first_limit // kb >= -(-(n_past + (i + 1) * tq) // kb) - 1, (tq, kb, n_past, i)
    stream = lambda n, w: pl.BlockSpec((1, n, w), lambda bi, i: (bi, 0, 0))
    past_c = pl.BlockSpec((None, 1, n_past, KV_RANK), lambda bi, i: (l, bi, 0, 0))
    past_r = pl.BlockSpec((None, 1, DR, n_past), lambda bi, i: (l, bi, 0, 0))
    in_specs = [pl.BlockSpec((H_A, tq, DQK), lambda bi, i: (0, bi * nq + i, 0)), stream(t, KV_RANK), stream(t, DR)]
    args = [q, ckv, kr]
    if has_past:
        in_specs += [past_c, past_r]
        args += [past_ckv, past_kr]
    in_specs.append(_layer_spec(w_uv_t, l))
    args.append(w_uv_t)
    return pl.pallas_call(
        functools.partial(_mla_kernel, tq=tq, kb=kb, n_past=n_past, t_new=t, has_past=has_past),
        grid=(b, nq),
        in_specs=in_specs,
        out_specs=pl.BlockSpec((1, tq, H_A * DV), lambda bi, i: (bi, i, 0)),
        out_shape=jax.ShapeDtypeStruct((b, t, H_A * DV), BF16),
        scratch_shapes=[pltpu.VMEM((t_pad, DQK), BF16), pltpu.VMEM((KV_RANK, t_pad), BF16)],
        compiler_params=_cparams("parallel", "arbitrary"),
        name="mla_attention",
    )(*args)


def _split3(x):
    hi = x.astype(BF16)
    r1 = x - hi.astype(F32)
    mid = r1.astype(BF16)
    lo = (r1 - mid.astype(F32)).astype(BF16)
    return hi, mid, lo


def _wkv_kernel(pb_ref, sh0_ref, s0_ref, mu_ref, w0_ref, wup_ref, a0_ref, aup_ref, gup_ref, kk_ref, ka_ref,
                rk_ref, gng_ref, gnb_ref, seg_ref, y_ref, so_ref, st_scr, prev_scr, *, sb):
    c = pl.program_id(1)
    nb = sb * H_B
    n = sb * CHUNK

    @pl.when(c == 0)
    def _():
        st_scr[...] = jnp.swapaxes(s0_ref[...].reshape(nb, N_B, N_B), 1, 2)
        prev_scr[...] = sh0_ref[...]

    row = lax.broadcasted_iota(jnp.int32, (CHUNK, 1), 0)
    parts = []
    for s in range(sb):
        pb = pb_ref[s]
        prev = jnp.where(row == 0, prev_scr[s], pltpu.roll(pb, shift=1, axis=0))
        prev_scr[s] = pb[CHUNK - 1:, :]
        parts.append(pb + mu_ref[...] * (prev - pb))
    xs = jnp.concatenate(parts, axis=0)
    r = xs[:, :D_B]
    k = xs[:, D_B:2 * D_B]
    v = xs[:, 2 * D_B:3 * D_B]
    o = 3 * D_B
    xw, xa, xg = xs[:, o:o + W_RANK], xs[:, o + W_RANK:o + W_RANK + A_RANK], xs[:, o + W_RANK + A_RANK:]
    z = w0_ref[...] + _dot(jnp.tanh(xw).astype(BF16), wup_ref[...])
    logw = -math.exp(-0.5) * jax.nn.sigmoid(z)
    gate = jax.nn.sigmoid(a0_ref[...] + _dot(xa.astype(BF16), aup_ref[...]))
    g = _dot(jax.nn.sigmoid(xg).astype(BF16), gup_ref[...])
    kkf = k * kk_ref[...]
    k2 = k * (1.0 + (gate - 1.0) * ka_ref[...])
    ri = lax.broadcasted_iota(jnp.int32, (n, 1), 0)
    ci = lax.broadcasted_iota(jnp.int32, (1, n), 1)
    same_stream = jnp.right_shift(ri, 6) == jnp.right_shift(ci, 6)
    ones_tri = jnp.where(ri >= ci, jnp.where(same_stream, 1.0, 0.0), 0.0).astype(BF16)
    cs = sum(_dot(ones_tri, piece) for piece in _split3(logw))
    p_in = jnp.exp(cs)
    p_prev = jnp.exp(cs - logw)
    p_inv = jnp.exp(-cs)

    ones_blk = seg_ref[...]

    def seg_sum(x):
        hi = x.astype(BF16)
        lo = (x - hi.astype(F32)).astype(BF16)
        return _dot(hi, ones_blk) + _dot(lo, ones_blk)

    def heads3(x):
        return jnp.stack([x[s * CHUNK:(s + 1) * CHUNK, h * N_B:(h + 1) * N_B]
                          for s in range(sb) for h in range(H_B)], axis=0)

    kk = kkf * lax.rsqrt(seg_sum(kkf * kkf) + 1e-12)
    bonus = seg_sum(r * k2 * rk_ref[...]) * v
    at32, rt32 = heads3(-kk * p_prev), heads3(r * p_in)
    kt32, bt32 = heads3(k2 * p_inv), heads3(kk * gate * p_inv)
    v3 = heads3(v)
    pend3 = jnp.stack([p_in[(s + 1) * CHUNK - 1:(s + 1) * CHUNK, h * N_B:(h + 1) * N_B]
                       for s in range(sb) for h in range(H_B)], axis=0)
    at, rt = at32.astype(BF16), rt32.astype(BF16)
    bk = jnp.concatenate([bt32, kt32], axis=1).astype(BF16)
    bhkh_t = jnp.swapaxes(jnp.concatenate([bt32 * pend3, kt32 * pend3], axis=1), 1, 2).astype(BF16)
    vb = v3.astype(BF16)

    tt = lax.broadcasted_iota(jnp.int32, (1, CHUNK, 2 * CHUNK), 1)
    ss = lax.broadcasted_iota(jnp.int32, (1, CHUNK, 2 * CHUNK), 2) & (CHUNK - 1)
    a_top = jnp.where(tt > ss, _bmm_nt(at, bk), 0.0)
    a_bot = jnp.where(tt >= ss, _bmm_nt(rt, bk), 0.0).astype(BF16)
    zero_v = jnp.concatenate([jnp.zeros((nb, CHUNK, N_B), BF16), vb], axis=1)
    zz = jnp.concatenate([at32, _bmm(a_top.astype(BF16), zero_v)], axis=-1)
    x = a_top[:, :, :CHUNK]
    for lvl in range(6):
        xb = x.astype(BF16)
        if lvl < 5:
            res = _bmm(xb, jnp.concatenate([zz.astype(BF16), xb], axis=-1))
            zz = zz + res[:, :, :2 * N_B]
            x = res[:, :, 2 * N_B:]
        else:
            zz = zz + _bmm(xb, zz.astype(BF16))
    w_m, u0 = zz[:, :, :N_B].astype(BF16), zz[:, :, N_B:].astype(BF16)
    res2 = _bmm(jnp.concatenate([a_bot[:, :, :CHUNK], bhkh_t[:, :, :CHUNK]], axis=1), w_m)
    q_hat = rt32 + res2[:, :CHUNK]
    eye = lax.broadcasted_iota(jnp.int32, (1, N_B, N_B), 1) == lax.broadcasted_iota(jnp.int32, (1, N_B, N_B), 2)
    m_t = jnp.where(eye, pend3, 0.0) + res2[:, CHUNK:]
    lhs = jnp.concatenate([jnp.concatenate([a_bot, q_hat.astype(BF16)], axis=-1),
                           jnp.concatenate([bhkh_t, m_t.astype(BF16)], axis=-1)], axis=1)
    rhs = jnp.concatenate([u0, vb, st_scr[...].astype(BF16)], axis=1)
    res3 = _bmm(lhs, rhs)
    y3 = res3[:, :CHUNK]
    st_scr[...] = res3[:, CHUNK:]

    y = jnp.concatenate([jnp.concatenate([y3[s * H_B + h] for h in range(H_B)], axis=-1) for s in range(sb)], axis=0)
    d = y - seg_sum(y) * (1.0 / N_B)
    var = seg_sum(d * d) * (1.0 / N_B)
    y = (d * lax.rsqrt(var + GN_EPS) * gng_ref[...] + gnb_ref[...] + bonus) * g
    for s in range(sb):
        y_ref[s] = y[s * CHUNK:(s + 1) * CHUNK].astype(y_ref.dtype)

    @pl.when(c == pl.num_programs(1) - 1)
    def _():
        so_ref[...] = jnp.swapaxes(st_scr[...], 1, 2).reshape(sb, H_B, N_B, N_B)


def wkv_group(pb, l, ls, shift0, s0, mu, w0, w_up, a0, a_up, g_up, k_k, k_a, r_k, gn_g, gn_b):
    b, t, _ = pb.shape
    sb = min(WKV_STREAMS, b)
    small = [mu, w0, w_up, a0, a_up, g_up, k_k, k_a, r_k, gn_g, gn_b]
    head_of = jnp.arange(D_B, dtype=jnp.int32) // N_B
    same_head = (head_of[:, None] == head_of[None, :]).astype(BF16)
    return pl.pallas_call(
        functools.partial(_wkv_kernel, sb=sb),
        grid=(b // sb, t // CHUNK),
        in_specs=[pl.BlockSpec((sb, CHUNK, D_SHIFT), lambda bi, c: (bi, c, 0)),
                  pl.BlockSpec((None, sb, 1, D_SHIFT), lambda bi, c: (ls, bi, 0, 0)),
                  pl.BlockSpec((None, sb, H_B, N_B, N_B), lambda bi, c: (ls, bi, 0, 0, 0))]
        + [_layer_spec(a, l) for a in small]
        + [pl.BlockSpec((D_B, D_B), lambda bi, c: (0, 0), pipeline_mode=pl.Buffered(1))],
        out_specs=[pl.BlockSpec((sb, CHUNK, D_B), lambda bi, c: (bi, c, 0)),
                   pl.BlockSpec((sb, H_B, N_B, N_B), lambda bi, c: (bi, 0, 0, 0))],
        out_shape=[jax.ShapeDtypeStruct((b, t, D_B), BF16), jax.ShapeDtypeStruct((b, H_B, N_B, N_B), F32)],
        scratch_shapes=[pltpu.VMEM((sb * H_B, N_B, N_B), F32), pltpu.VMEM((sb, 1, D_SHIFT), F32)],
        compiler_params=_cparams("parallel", "arbitrary"),
        name="wkv_group",
    )(pb, shift0, s0, *small, same_head)


def _out_kernel(x_ref, ya_ref, yb_ref, wo_ref, g_ref, wq_ref, mk_ref, mv_ref, wmo_ref, o_ref, *, spt):
    x = x_ref[...] + _dot(ya_ref[...], wo_ref[:H_A * DV, :]) + _dot(yb_ref[...], wo_ref[H_A * DV:, :])
    q = _dot(_rms(x, g_ref[...]).astype(BF16), wq_ref[...])
    ts = x.shape[0] // spt
    halves = lambda hd: [pl.ds(j * MEM_HEADS + hd, N_MEM, stride=MEM_ROWS) for j in range(MEM_LANE_HALVES)]
    pairs = [(s, hd) for s in range(spt) for hd in range(MEM_HEADS)]
    scores = []
    for s, hd in pairs:
        qh = q[s * ts:(s + 1) * ts, hd * MEM_HD:(hd + 1) * MEM_HD].astype(BF16)
        scores.append(sum(_dot_nt(qh[:, j * 128:(j + 1) * 128], mk_ref[s, rws, :].astype(BF16))
                          for j, rws in enumerate(halves(hd))) / math.sqrt(MEM_HD))
    probs = []
    for sc in scores:
        sc = jnp.exp(sc - jnp.max(sc, axis=-1, keepdims=True))
        probs.append((sc / jnp.sum(sc, axis=-1, keepdims=True)).astype(BF16))
    outs = {}
    for (s, hd), p in zip(pairs, probs):
        outs[s, hd] = [_dot(p, mv_ref[s, rws, :].astype(BF16)) for rws in halves(hd)]
    rows = []
    for s in range(spt):
        rows.append(jnp.concatenate([o for hd in range(MEM_HEADS) for o in outs[s, hd]], axis=-1))
    att = rows[0] if spt == 1 else jnp.concatenate(rows, axis=0)
    o_ref[...] = x + _dot(att.astype(BF16), wmo_ref[...])


def mixer_out_xattn(x, ya, yb, l, w_out, g, w_mq, lm, mk, mv, w_mo, *, t_stream):
    n, d = x.shape
    tm = min(TOKEN_TILE, n)
    spt = max(1, tm // t_stream)
    tps = max(1, t_stream // tm)
    tok = lambda w: pl.BlockSpec((tm, w), lambda i: (i, 0))
    mem = pl.BlockSpec((None, spt, N_MEM * MEM_ROWS, 128), lambda i: (lm, i // tps, 0, 0))
    return pl.pallas_call(
        functools.partial(_out_kernel, spt=spt),
        grid=(n // tm,),
        in_specs=[tok(d), tok(H_A * DV), tok(D_B), _layer_spec(w_out, l), _layer_spec(g, l),
                  _layer_spec(w_mq, l), mem, mem, _layer_spec(w_mo, l)],
        out_specs=tok(d),
        out_shape=jax.ShapeDtypeStruct((n, d), F32),
        compiler_params=_cparams("parallel"),
        name="mixer_out_xattn",
    )(x, ya, yb, w_out, g, w_mq, mk, mv, w_mo)


def _memkv_kernel(m_ref, g_ref, wk_ref, wv_ref, k_ref, v_ref):
    tm = m_ref.shape[0]
    m = _rms(m_ref[...], g_ref[...]).astype(BF16)
    for w_ref, o_ref in ((wk_ref, k_ref), (wv_ref, v_ref)):
        kv = _dot(m, w_ref[...])
        for hd in range(MEM_HEADS):
            for j in range(MEM_LANE_HALVES):
                col = hd * MEM_HD + j * 128
                o_ref[pl.ds(j * MEM_HEADS + hd, tm, stride=MEM_ROWS), :] = kv[:, col:col + 128]


def mem_kv(mem, l, g, w_mk, w_mv):
    n, d = mem.shape
    tm = min(TOKEN_TILE, n)
    rows = pl.BlockSpec((tm * MEM_ROWS, 128), lambda i: (i, 0))
    return pl.pallas_call(
        _memkv_kernel,
        grid=(n // tm,),
        in_specs=[pl.BlockSpec((tm, d), lambda i: (i, 0)), _layer_spec(g, l), _layer_spec(w_mk, l),
                  _layer_spec(w_mv, l)],
        out_specs=[rows, rows],
        out_shape=[jax.ShapeDtypeStruct((n * MEM_ROWS, 128), F32)] * 2,
        compiler_params=_cparams("parallel"),
        name="mem_kv",
    )(mem, g, w_mk, w_mv)


def _rot_half_cols(w):
    half = DR // 2
    shp = w.shape
    w = w.reshape(shp[:-1] + (shp[-1] // DR, 2, half))
    return jnp.concatenate([-w[..., 1:, :], w[..., :1, :]], axis=-2).reshape(shp)


def _mem_rows(a):
    lead = a.shape[:-3]
    a = a.reshape(lead + (N_MEM, MEM_HEADS, MEM_LANE_HALVES, 128))
    return jnp.swapaxes(a, -2, -3).reshape(lead + (N_MEM * MEM_ROWS, 128))


def _mem_heads(a):
    lead = a.shape[:-2]
    a = a.reshape(lead + (N_MEM, MEM_LANE_HALVES, MEM_HEADS, 128))
    return jnp.swapaxes(a, -2, -3).reshape(lead + (N_MEM, MEM_HEADS, MEM_HD))


def _rope_tables(pos, reps):
    half = DR // 2
    inv = ROPE_THETA ** (-jnp.arange(half, dtype=F32) / half)
    ang = pos.astype(F32)[:, None] * inv[None, :]
    cos = jnp.tile(jnp.cos(ang), (1, 2 * reps))
    sin = jnp.tile(jnp.sin(ang), (1, 2 * reps))
    return cos, sin


def _group_tables(pos, n_tokens):
    rows = max(pos.shape[0], min(TOKEN_TILE, n_tokens))
    pos = jnp.tile(pos, rows // pos.shape[0])
    return _rope_tables(pos, H_A) + _rope_tables(pos, 1)


def _layer_group(x, tabs, lw, l, b, t, past_ckv, past_kr, ls, wkv0, shift0, lm, mk, mv, *, tq, kb, final_norm):
    x = ffn_half(x, l, lw["ffn1_norm"], lw["ffn1_wg"], lw["ffn1_wu"], lw["ffn1_wd"], lw["final_norm"],
                 final_norm=False)
    q, ckv, kr, pb = mixer_proj(x, l, lw["mix_norm"], lw["w_in_ext"], lw["q_norm"], lw["w_uq_ext"], lw["w_uk"],
                                lw["kv_norm"], *tabs)
    ckv3, kr3 = ckv.reshape(b, t, KV_RANK), kr.reshape(b, t, DR)
    ya = mla_attention(q, ckv3, kr3, l, past_ckv, past_kr, lw["w_uv_t"], tq=tq, kb=kb)
    pb3 = pb.reshape(b, t, D_SHIFT)
    yb, wkv = wkv_group(pb3, l, ls, shift0, wkv0, *(lw[n] for n in (
        "shift_mu", "w0", "w_up", "a0", "a_up", "g_up", "k_k", "k_a", "r_k", "gn_gain", "gn_bias")))
    x = mixer_out_xattn(x, ya.reshape(b * t, H_A * DV), yb.reshape(b * t, D_B), l, lw["w_out"], lw["xattn_norm"],
                        lw["w_mq"], lm, mk, mv, lw["w_mo"], t_stream=t)
    x = ffn_half(x, l, lw["ffn2_norm"], lw["ffn2_wg"], lw["ffn2_wu"], lw["ffn2_wd"], lw["final_norm"],
                 final_norm=final_norm)
    return x, ckv3, kr3, wkv, pb3[:, -1:, :]


def kernel(x_prompt, x_sample, mem_prompt, cache_ckv, cache_krope, cache_mem_k, cache_mem_v, state_wkv, state_shift, ffn1_norm, ffn1_w_gate, ffn1_w_up, ffn1_w_down, mix_norm, w_in, q_norm, w_uq, kv_norm, w_uk, w_uv, shift_mu, w0, w_up, a0, a_up, g_up, k_k, k_a, r_k, gn_gain, gn_bias, w_out, xattn_norm, mem_kv_norm, w_mq, w_mk, w_mv, w_mo, ffn2_norm, ffn2_w_gate, ffn2_w_up, ffn2_w_down, final_norm):
    depth = w_in.shape[0]
    b_p, t_p, d = x_prompt.shape
    b_s, t_s, _ = x_sample.shape
    n_past = cache_ckv.shape[2]
    bf = lambda a: a.astype(BF16)
    vec = lambda a: a.reshape(depth, 1, -1)

    o = Q_RANK + KV_RANK
    w_kr = w_in[..., o:o + DR]
    w_in_ext = jnp.concatenate(
        [w_in[..., :o + DR], _rot_half_cols(w_kr), jnp.zeros(w_in.shape[:2] + (PROJ_HEAD - o - 2 * DR,), w_in.dtype),
         w_in[..., o + DR:]], axis=-1)
    uq = w_uq.reshape(depth, Q_RANK, H_A, DN + DR)
    uq_nope = uq[..., :DN].reshape(depth, Q_RANK, H_A * DN)
    uq_rope = uq[..., DN:].reshape(depth, Q_RANK, H_A * DR)
    w_uq_ext = jnp.concatenate([uq_nope, uq_rope, _rot_half_cols(uq_rope)], axis=-1)
    lw = dict(
        ffn1_norm=vec(ffn1_norm), ffn1_wg=bf(ffn1_w_gate), ffn1_wu=bf(ffn1_w_up), ffn1_wd=bf(ffn1_w_down),
        mix_norm=vec(mix_norm), w_in_ext=bf(w_in_ext), q_norm=vec(q_norm), w_uq_ext=bf(w_uq_ext),
        kv_norm=vec(kv_norm), w_uk=bf(w_uk), w_uv_t=bf(jnp.swapaxes(w_uv, -1, -2)), shift_mu=vec(shift_mu),
        w0=vec(w0), w_up=bf(w_up), a0=vec(a0), a_up=bf(a_up), g_up=bf(g_up), k_k=vec(k_k), k_a=vec(k_a),
        r_k=vec(r_k), gn_gain=vec(gn_gain), gn_bias=vec(gn_bias), w_out=bf(w_out), xattn_norm=vec(xattn_norm),
        w_mq=bf(w_mq), w_mo=bf(w_mo), ffn2_norm=vec(ffn2_norm), ffn2_wg=bf(ffn2_w_gate), ffn2_wu=bf(ffn2_w_up),
        ffn2_wd=bf(ffn2_w_down), final_norm=final_norm.reshape(1, 1, d))
    mem_norm, w_mk_b, w_mv_b = vec(mem_kv_norm), bf(w_mk), bf(w_mv)

    tabs_p = _group_tables(jnp.arange(t_p, dtype=jnp.int32), b_p * t_p)
    tabs_s = _group_tables(n_past + jnp.arange(t_s, dtype=jnp.int32), b_s * t_s)
    wkv_zero = jnp.zeros((1, b_p, H_B, N_B, N_B), x_prompt.dtype)
    shift_zero = jnp.zeros((1, b_p, 1, D_SHIFT), x_prompt.dtype)
    krope_t = jnp.swapaxes(cache_krope, -1, -2)
    mem_k_s, mem_v_s = _mem_rows(cache_mem_k), _mem_rows(cache_mem_v)
    kb_s = -(-(n_past + t_s) // (3 * 128)) * 128

    xp = x_prompt.reshape(b_p * t_p, d)
    xs = x_sample.reshape(b_s * t_s, d)
    mem2 = mem_prompt.reshape(b_p * N_MEM, d)
    outs_p = [[] for _ in range(6)]
    outs_s = [[] for _ in range(4)]
    for l in range(depth):
        last = l == depth - 1
        mk, mv = (a.reshape(1, b_p, N_MEM * MEM_ROWS, 128) for a in mem_kv(mem2, l, mem_norm, w_mk_b, w_mv_b))
        xp, c1, k1, s1, h1 = _layer_group(xp, tabs_p, lw, l, b_p, t_p, None, None, 0, wkv_zero, shift_zero, 0,
                                          mk, mv, tq=min(512, t_p), kb=512, final_norm=last)
        for acc, val in zip(outs_p, (c1, k1, mk[0], mv[0], s1, h1)):
            acc.append(val)
        xs, c2, k2, s2, h2 = _layer_group(xs, tabs_s, lw, l, b_s, t_s, cache_ckv, krope_t, l, state_wkv,
                                          state_shift, l, mem_k_s, mem_v_s, tq=t_s, kb=kb_s, final_norm=last)
        for acc, val in zip(outs_s, (c2, k2, s2, h2)):
            acc.append(val)
    stack = lambda seq: jnp.stack(seq)
    return (xp.reshape(b_p, t_p, d), xs.reshape(b_s, t_s, d),
            stack(outs_p[0]), stack(outs_p[1]), _mem_heads(stack(outs_p[2])), _mem_heads(stack(outs_p[3])),
            stack(outs_p[4]), stack(outs_p[5]),
            stack(outs_s[0]), stack(outs_s[1]), stack(outs_s[2]), stack(outs_s[3]))
```

```python
import functools
import math

import jax
import jax.numpy as jnp
from jax import lax
from jax.experimental import pallas as pl
from jax.experimental.pallas import tpu as pltpu

F32 = jnp.float32
BF16 = jnp.bfloat16

D_MODEL = 1024
CHUNK = 64
H_A, DN, DR, DV = 8, 64, 32, 64
Q_RANK, KV_RANK = 256, 128
DQK = KV_RANK + DR
ROPE_THETA = 10000.0
MLA_SCALE = 1.0 / math.sqrt(DN + DR)
LOG2E = math.log2(math.e)
H_B, N_B = 8, 64
D_B = H_B * N_B
W_RANK, A_RANK, G_RANK = 64, 64, 128
D_SHIFT = 3 * D_B + W_RANK + A_RANK + G_RANK
N_MEM, MEM_HEADS = 256, 4
MEM_HD = D_MODEL // MEM_HEADS
MEM_LANE_HALVES = MEM_HD // 128
MEM_ROWS = MEM_LANE_HALVES * MEM_HEADS
D_FF = 2816
EPS = 1e-6
GN_EPS = 64e-5
NEG_INF = -1e30

PROJ_HEAD = 512
D_PROJ = PROJ_HEAD + D_SHIFT

TOKEN_TILE = 512
WKV_STREAMS = 4
VMEM_LIMIT = 56 * 1024 * 1024


def _cparams(*sem):
    return pltpu.CompilerParams(dimension_semantics=sem, vmem_limit_bytes=VMEM_LIMIT)


def _layer_spec(arr, l):
    nd = arr.ndim - 1
    return pl.BlockSpec((None,) + arr.shape[1:], lambda *_: (l,) + (0,) * nd, pipeline_mode=pl.Buffered(1))


def _rms(x, g):
    return x * lax.rsqrt(jnp.mean(x * x, axis=-1, keepdims=True) + EPS) * g


def _dot(a, b):
    return jnp.dot(a, b, preferred_element_type=F32)


def _dot_nt(a, b):
    return lax.dot_general(a, b, (((1,), (1,)), ((), ())), preferred_element_type=F32)


def _bmm(a, b):
    return jnp.einsum("gmk,gkn->gmn", a, b, preferred_element_type=F32)


def _bmm_nt(a, b):
    return jnp.einsum("gmk,gnk->gmn", a, b, preferred_element_type=F32)


def _ffn_kernel(x_ref, g_ref, wg_ref, wu_ref, wd_ref, gf_ref, o_ref, *, final_norm):
    x = x_ref[...]
    h = _rms(x, g_ref[...]).astype(BF16)
    gate = _dot(h, wg_ref[...])
    up = _dot(h, wu_ref[...])
    act = (gate * jax.nn.sigmoid(gate) * up).astype(BF16)
    y = x + 0.5 * _dot(act, wd_ref[...])
    if final_norm:
        y = _rms(y, gf_ref[...])
    o_ref[...] = y


def ffn_half(x, l, g, wg, wu, wd, gf, *, final_norm):
    n, d = x.shape
    tm = min(TOKEN_TILE, n)
    return pl.pallas_call(
        functools.partial(_ffn_kernel, final_norm=final_norm),
        grid=(n // tm,),
        in_specs=[pl.BlockSpec((tm, d), lambda i: (i, 0)), _layer_spec(g, l), _layer_spec(wg, l),
                  _layer_spec(wu, l), _layer_spec(wd, l), _layer_spec(gf, 0)],
        out_specs=pl.BlockSpec((tm, d), lambda i: (i, 0)),
        out_shape=jax.ShapeDtypeStruct((n, d), F32),
        compiler_params=_cparams("parallel"),
        name="ffn_half",
    )(x, g, wg, wu, wd, gf)


def _proj_kernel(x_ref, g_ref, win_ref, qn_ref, wuq_ref, wuk_ref, kvn_ref, cq_ref, sq_ref, ck_ref, sk_ref,
                 q_ref, ckv_ref, kr_ref, pb_ref):
    h = _rms(x_ref[...], g_ref[...]).astype(BF16)
    proj = _dot(h, win_ref[...])
    c_q = proj[:, :Q_RANK]
    c_kv = proj[:, Q_RANK:Q_RANK + KV_RANK]
    o = Q_RANK + KV_RANK
    k_r, k_rot = proj[:, o:o + DR], proj[:, o + DR:o + 2 * DR]
    ckv_ref[...] = _rms(c_kv, kvn_ref[...])
    kr_ref[...] = k_r * ck_ref[...] + k_rot * sk_ref[...]
    pb_ref[...] = proj[:, PROJ_HEAD:]
    qf = _dot(_rms(c_q, qn_ref[...]).astype(BF16), wuq_ref[...])
    nq = H_A * DN
    q_rope = (qf[:, nq:nq + H_A * DR] * cq_ref[...] + qf[:, nq + H_A * DR:] * sq_ref[...]) * (MLA_SCALE * LOG2E)
    for hd in range(H_A):
        q_lat = _dot_nt(qf[:, hd * DN:(hd + 1) * DN].astype(BF16), wuk_ref[hd]) * (MLA_SCALE * LOG2E)
        q_ref[hd, :, :KV_RANK] = q_lat.astype(BF16)
        q_ref[hd, :, KV_RANK:] = q_rope[:, hd * DR:(hd + 1) * DR].astype(BF16)


def mixer_proj(x, l, g, w_in_ext, q_norm, w_uq_ext, w_uk, kv_norm, cos_q, sin_q, cos_k, sin_k):
    n, d = x.shape
    tm = min(TOKEN_TILE, n)
    n_tab = cos_q.shape[0] // tm
    tok = lambda w: pl.BlockSpec((tm, w), lambda i: (i, 0))
    tab = lambda w: pl.BlockSpec((tm, w), lambda i: (i % n_tab, 0))
    return pl.pallas_call(
        _proj_kernel,
        grid=(n // tm,),
        in_specs=[tok(d), _layer_spec(g, l), _layer_spec(w_in_ext, l), _layer_spec(q_norm, l),
                  _layer_spec(w_uq_ext, l), _layer_spec(w_uk, l), _layer_spec(kv_norm, l),
                  tab(H_A * DR), tab(H_A * DR), tab(DR), tab(DR)],
        out_specs=[pl.BlockSpec((H_A, tm, DQK), lambda i: (0, i, 0)), tok(KV_RANK), tok(DR), tok(D_SHIFT)],
        out_shape=[jax.ShapeDtypeStruct((H_A, n, DQK), BF16), jax.ShapeDtypeStruct((n, KV_RANK), F32),
                   jax.ShapeDtypeStruct((n, DR), F32), jax.ShapeDtypeStruct((n, D_SHIFT), F32)],
        compiler_params=_cparams("parallel"),
        name="mixer_proj",
    )(x, g, w_in_ext, q_norm, w_uq_ext, w_uk, kv_norm, cos_q, sin_q, cos_k, sin_k)


def _mla_kernel(*refs, tq, kb, n_past, t_new, has_past):
    if has_past:
        q_ref, ckv_ref, kr_ref, pckv_ref, pkr_ref, wuvt_ref, o_ref, kcat, vt = refs
    else:
        q_ref, ckv_ref, kr_ref, wuvt_ref, o_ref, kcat, vt = refs
        pckv_ref = pkr_ref = None
    i = pl.program_id(1)
    t_all = n_past + t_new
    t_pad = kcat.shape[0]

    @pl.when(i == 0)
    def _():
        if t_pad > t_all:
            kcat[t_all:, :] = jnp.zeros((t_pad - t_all, DQK), BF16)
            vt[:, t_all:] = jnp.zeros((KV_RANK, t_pad - t_all), BF16)
        pieces = [(0, pckv_ref, pkr_ref, n_past)] if has_past else []
        for base, c_ref, r_ref, n in pieces + [(n_past, ckv_ref, kr_ref, t_new)]:
            kcat[base:base + n, :KV_RANK] = c_ref[0].astype(BF16)
            step_t = min(n, 512)
            for o in range(0, n, step_t):
                vt[:, base + o:base + o + step_t] = c_ref[0, o:o + step_t, :].T.astype(BF16)
                if r_ref is pkr_ref:
                    kcat[base + o:base + o + step_t, KV_RANK:] = r_ref[0, :, o:o + step_t].T.astype(BF16)
            if r_ref is not pkr_ref:
                kcat[base:base + n, KV_RANK:] = r_ref[0].astype(BF16)

    rows = H_A * tq
    q = q_ref[...].reshape(rows, DQK)
    t_in = lax.broadcasted_iota(jnp.int32, (1, rows), 1) & (tq - 1)
    limit = ((n_past + i * tq + t_in) // CHUNK + 1) * CHUNK
    n_blocks = (n_past + (i + 1) * tq + kb - 1) // kb

    def scores(j):
        return _dot_nt(kcat[pl.ds(pl.multiple_of(j * kb, kb), kb), :], q)

    def update(j, carry, st, masked):
        m, l, acc = carry
        if masked:
            k_pos = j * kb + lax.broadcasted_iota(jnp.int32, (kb, 1), 0)
            st = jnp.where(k_pos < limit, st, NEG_INF)
        m_new = jnp.maximum(m, jnp.max(st, axis=0, keepdims=True))
        alpha = jnp.exp2(m - m_new)
        p = jnp.exp2(st - m_new)
        l = alpha * l + jnp.sum(p, axis=0, keepdims=True)
        acc = alpha * acc + _dot(vt[:, pl.ds(pl.multiple_of(j * kb, kb), kb)], p.astype(BF16))
        return m_new, l, acc

    def pair(jj, carry):
        st0, st1 = scores(2 * jj), scores(2 * jj + 1)
        return update(2 * jj + 1, update(2 * jj, carry, st0, False), st1, False)

    def single(j, carry):
        return update(j, carry, scores(j), False)

    m0 = jnp.full((1, rows), NEG_INF, F32)
    l0 = jnp.zeros((1, rows), F32)
    a0 = jnp.zeros((KV_RANK, rows), F32)
    n_pairs = (n_blocks - 1) // 2
    carry = lax.fori_loop(0, n_pairs, pair, (m0, l0, a0))
    carry = lax.fori_loop(2 * n_pairs, n_blocks - 1, single, carry)
    _, l, acc = update(n_blocks - 1, carry, scores(n_blocks - 1), True)
    o_lat = (acc / l).astype(BF16)
    outs = [_dot(wuvt_ref[hd], o_lat[:, hd * tq:(hd + 1) * tq]) for hd in range(H_A)]
    o_ref[0] = jnp.concatenate(outs, axis=0).T.astype(o_ref.dtype)


def mla_attention(q, ckv, kr, l, past_ckv, past_kr, w_uv_t, *, tq, kb):
    b, t, _ = ckv.shape
    has_past = past_ckv is not None
    n_past = past_ckv.shape[2] if has_past else 0
    t_pad = -(-(n_past + t) // kb) * kb
    nq = t // tq
    for i in range(nq):
        first_limit = ((n_past + i * tq) // CHUNK + 1) * CHUNK
        assert first_limit // kb >= -(-(n_past + (i + 1) * tq) // kb) - 1, (tq, kb, n_past, i)
    stream = lambda n, w: pl.BlockSpec((1, n, w), lambda bi, i: (bi, 0, 0))
    past_c = pl.BlockSpec((None, 1, n_past, KV_RANK), lambda bi, i: (l, bi, 0, 0))
    past_r = pl.BlockSpec((None, 1, DR, n_past), lambda bi, i: (l, bi, 0, 0))
    in_specs = [pl.BlockSpec((H_A, tq, DQK), lambda bi, i: (0, bi * nq + i, 0)), stream(t, KV_RANK), stream(t, DR)]
    args = [q, ckv, kr]
    if has_past:
        in_specs += [past_c, past_r]
        args += [past_ckv, past_kr]
    in_specs.append(_layer_spec(w_uv_t, l))
    args.append(w_uv_t)
    return pl.pallas_call(
        functools.partial(_mla_kernel, tq=tq, kb=kb, n_past=n_past, t_new=t, has_past=has_past),
        grid=(b, nq),
        in_specs=in_specs,
        out_specs=pl.BlockSpec((1, tq, H_A * DV), lambda bi, i: (bi, i, 0)),
        out_shape=jax.ShapeDtypeStruct((b, t, H_A * DV), BF16),
        scratch_shapes=[pltpu.VMEM((t_pad, DQK), BF16), pltpu.VMEM((KV_RANK, t_pad), BF16)],
        compiler_params=_cparams("parallel", "arbitrary"),
        name="mla_attention",
    )(*args)


def _split3(x):
    hi = x.astype(BF16)
    r1 = x - hi.astype(F32)
    mid = r1.astype(BF16)
    lo = (r1 - mid.astype(F32)).astype(BF16)
    return hi, mid, lo


def _wkv_kernel(pb_ref, sh0_ref, s0_ref, mu_ref, w0_ref, wup_ref, a0_ref, aup_ref, gup_ref, kk_ref, ka_ref,
                rk_ref, gng_ref, gnb_ref, seg_ref, y_ref, so_ref, st_scr, prev_scr, *, sb):
    c = pl.program_id(1)
    nb = sb * H_B
    n = sb * CHUNK

    @pl.when(c == 0)
    def _():
        st_scr[...] = jnp.swapaxes(s0_ref[...].reshape(nb, N_B, N_B), 1, 2)
        prev_scr[...] = sh0_ref[...]

    row = lax.broadcasted_iota(jnp.int32, (CHUNK, 1), 0)
    parts = []
    for s in range(sb):
        pb = pb_ref[s]
        prev = jnp.where(row == 0, prev_scr[s], pltpu.roll(pb, shift=1, axis=0))
        prev_scr[s] = pb[CHUNK - 1:, :]
        parts.append(pb + mu_ref[...] * (prev - pb))
    xs = jnp.concatenate(parts, axis=0)
    r = xs[:, :D_B]
    k = xs[:, D_B:2 * D_B]
    v = xs[:, 2 * D_B:3 * D_B]
    o = 3 * D_B
    xw, xa, xg = xs[:, o:o + W_RANK], xs[:, o + W_RANK:o + W_RANK + A_RANK], xs[:, o + W_RANK + A_RANK:]
    z = w0_ref[...] + _dot(jnp.tanh(xw).astype(BF16), wup_ref[...])
    logw = -math.exp(-0.5) * jax.nn.sigmoid(z)
    gate = jax.nn.sigmoid(a0_ref[...] + _dot(xa.astype(BF16), aup_ref[...]))
    g = _dot(jax.nn.sigmoid(xg).astype(BF16), gup_ref[...])
    kkf = k * kk_ref[...]
    k2 = k * (1.0 + (gate - 1.0) * ka_ref[...])
    ri = lax.broadcasted_iota(jnp.int32, (n, 1), 0)
    ci = lax.broadcasted_iota(jnp.int32, (1, n), 1)
    same_stream = jnp.right_shift(ri, 6) == jnp.right_shift(ci, 6)
    ones_tri = jnp.where(ri >= ci, jnp.where(same_stream, 1.0, 0.0), 0.0).astype(BF16)
    cs = sum(_dot(ones_tri, piece) for piece in _split3(logw))
    p_in = jnp.exp(cs)
    p_prev = jnp.exp(cs - logw)
    p_inv = jnp.exp(-cs)

    ones_blk = seg_ref[...]

    def seg_sum(x):
        hi = x.astype(BF16)
        lo = (x - hi.astype(F32)).astype(BF16)
        return _dot(hi, ones_blk) + _dot(lo, ones_blk)

    def heads3(x):
        return jnp.stack([x[s * CHUNK:(s + 1) * CHUNK, h * N_B:(h + 1) * N_B]
                          for s in range(sb) for h in range(H_B)], axis=0)

    kk = kkf * lax.rsqrt(seg_sum(kkf * kkf) + 1e-12)
    bonus = seg_sum(r * k2 * rk_ref[...]) * v
    at32, rt32 = heads3(-kk * p_prev), heads3(r * p_in)
    kt32, bt32 = heads3(k2 * p_inv), heads3(kk * gate * p_inv)
    v3 = heads3(v)
    pend3 = jnp.stack([p_in[(s + 1) * CHUNK - 1:(s + 1) * CHUNK, h * N_B:(h + 1) * N_B]
                       for s in range(sb) for h in range(H_B)], axis=0)
    at, rt = at32.astype(BF16), rt32.astype(BF16)
    bk = jnp.concatenate([bt32, kt32], axis=1).astype(BF16)
    bhkh_t = jnp.swapaxes(jnp.concatenate([bt32 * pend3, kt32 * pend3], axis=1), 1, 2).astype(BF16)
    vb = v3.astype(BF16)

    tt = lax.broadcasted_iota(jnp.int32, (1, CHUNK, 2 * CHUNK), 1)
    ss = lax.broadcasted_iota(jnp.int32, (1, CHUNK, 2 * CHUNK), 2) & (CHUNK - 1)
    a_top = jnp.where(tt > ss, _bmm_nt(at, bk), 0.0)
    a_bot = jnp.where(tt >= ss, _bmm_nt(rt, bk), 0.0).astype(BF16)
    zero_v = jnp.concatenate([jnp.zeros((nb, CHUNK, N_B), BF16), vb], axis=1)
    zz = jnp.concatenate([at32, _bmm(a_top.astype(BF16), zero_v)], axis=-1)
    x = a_top[:, :, :CHUNK]
    for lvl in range(6):
        xb = x.astype(BF16)
        if lvl < 5:
            res = _bmm(xb, jnp.concatenate([zz.astype(BF16), xb], axis=-1))
            zz = zz + res[:, :, :2 * N_B]
            x = res[:, :, 2 * N_B:]
        else:
            zz = zz + _bmm(xb, zz.astype(BF16))
    w_m, u0 = zz[:, :, :N_B].astype(BF16), zz[:, :, N_B:].astype(BF16)
    res2 = _bmm(jnp.concatenate([a_bot[:, :, :CHUNK], bhkh_t[:, :, :CHUNK]], axis=1), w_m)
    q_hat = rt32 + res2[:, :CHUNK]
    eye = lax.broadcasted_iota(jnp.int32, (1, N_B, N_B), 1) == lax.broadcasted_iota(jnp.int32, (1, N_B, N_B), 2)
    m_t = jnp.where(eye, pend3, 0.0) + res2[:, CHUNK:]
    lhs = jnp.concatenate([jnp.concatenate([a_bot, q_hat.astype(BF16)], axis=-1),
                           jnp.concatenate([bhkh_t, m_t.astype(BF16)], axis=-1)], axis=1)
    rhs = jnp.concatenate([u0, vb, st_scr[...].astype(BF16)], axis=1)
    res3 = _bmm(lhs, rhs)
    y3 = res3[:, :CHUNK]
    st_scr[...] = res3[:, CHUNK:]

    y = jnp.concatenate([jnp.concatenate([y3[s * H_B + h] for h in range(H_B)], axis=-1) for s in range(sb)], axis=0)
    d = y - seg_sum(y) * (1.0 / N_B)
    var = seg_sum(d * d) * (1.0 / N_B)
    y = (d * lax.rsqrt(var + GN_EPS) * gng_ref[...] + gnb_ref[...] + bonus) * g
    for s in range(sb):
        y_ref[s] = y[s * CHUNK:(s + 1) * CHUNK].astype(y_ref.dtype)

    @pl.when(c == pl.num_programs(1) - 1)
    def _():
        so_ref[...] = jnp.swapaxes(st_scr[...], 1, 2).reshape(sb, H_B, N_B, N_B)


def wkv_group(pb, l, ls, shift0, s0, mu, w0, w_up, a0, a_up, g_up, k_k, k_a, r_k, gn_g, gn_b):
    b, t, _ = pb.shape
    sb = min(WKV_STREAMS, b)
    small = [mu, w0, w_up, a0, a_up, g_up, k_k, k_a, r_k, gn_g, gn_b]
    head_of = jnp.arange(D_B, dtype=jnp.int32) // N_B
    same_head = (head_of[:, None] == head_of[None, :]).astype(BF16)
    return pl.pallas_call(
        functools.partial(_wkv_kernel, sb=sb),
        grid=(b // sb, t // CHUNK),
        in_specs=[pl.BlockSpec((sb, CHUNK, D_SHIFT), lambda bi, c: (bi, c, 0)),
                  pl.BlockSpec((None, sb, 1, D_SHIFT), lambda bi, c: (ls, bi, 0, 0)),
                  pl.BlockSpec((None, sb, H_B, N_B, N_B), lambda bi, c: (ls, bi, 0, 0, 0))]
        + [_layer_spec(a, l) for a in small]
        + [pl.BlockSpec((D_B, D_B), lambda bi, c: (0, 0), pipeline_mode=pl.Buffered(1))],
        out_specs=[pl.BlockSpec((sb, CHUNK, D_B), lambda bi, c: (bi, c, 0)),
                   pl.BlockSpec((sb, H_B, N_B, N_B), lambda bi, c: (bi, 0, 0, 0))],
        out_shape=[jax.ShapeDtypeStruct((b, t, D_B), BF16), jax.ShapeDtypeStruct((b, H_B, N_B, N_B), F32)],
        scratch_shapes=[pltpu.VMEM((sb * H_B, N_B, N_B), F32), pltpu.VMEM((sb, 1, D_SHIFT), F32)],
        compiler_params=_cparams("parallel", "arbitrary"),
        name="wkv_group",
    )(pb, shift0, s0, *small, same_head)


def _out_kernel(x_ref, ya_ref, yb_ref, wo_ref, g_ref, wq_ref, mk_ref, mv_ref, wmo_ref, o_ref, *, spt):
    x = x_ref[...] + _dot(ya_ref[...], wo_ref[:H_A * DV, :]) + _dot(yb_ref[...], wo_ref[H_A * DV:, :])
    q = _dot(_rms(x, g_ref[...]).astype(BF16), wq_ref[...])
    ts = x.shape[0] // spt
    halves = lambda hd: [pl.ds(j * MEM_HEADS + hd, N_MEM, stride=MEM_ROWS) for j in range(MEM_LANE_HALVES)]
    pairs = [(s, hd) for s in range(spt) for hd in range(MEM_HEADS)]
    scores = []
    for s, hd in pairs:
        qh = q[s * ts:(s + 1) * ts, hd * MEM_HD:(hd + 1) * MEM_HD].astype(BF16)
        scores.append(sum(_dot_nt(qh[:, j * 128:(j + 1) * 128], mk_ref[s, rws, :].astype(BF16))
                          for j, rws in enumerate(halves(hd))) / math.sqrt(MEM_HD))
    probs = []
    for sc in scores:
        sc = jnp.exp(sc - jnp.max(sc, axis=-1, keepdims=True))
        probs.append((sc / jnp.sum(sc, axis=-1, keepdims=True)).astype(BF16))
    outs = {}
    for (s, hd), p in zip(pairs, probs):
        outs[s, hd] = [_dot(p, mv_ref[s, rws, :].astype(BF16)) for rws in halves(hd)]
    rows = []
    for s in range(spt):
        rows.append(jnp.concatenate([o for hd in range(MEM_HEADS) for o in outs[s, hd]], axis=-1))
    att = rows[0] if spt == 1 else jnp.concatenate(rows, axis=0)
    o_ref[...] = x + _dot(att.astype(BF16), wmo_ref[...])


def mixer_out_xattn(x, ya, yb, l, w_out, g, w_mq, lm, mk, mv, w_mo, *, t_stream):
    n, d = x.shape
    tm = min(2 * TOKEN_TILE if t_stream >= 2 * TOKEN_TILE else TOKEN_TILE, n)
    spt = max(1, tm // t_stream)
    tps = max(1, t_stream // tm)
    tok = lambda w: pl.BlockSpec((tm, w), lambda i: (i, 0))
    mem = pl.BlockSpec((None, spt, N_MEM * MEM_ROWS, 128), lambda i: (lm, i // tps, 0, 0))
    return pl.pallas_call(
        functools.partial(_out_kernel, spt=spt),
        grid=(n // tm,),
        in_specs=[tok(d), tok(H_A * DV), tok(D_B), _layer_spec(w_out, l), _layer_spec(g, l),
                  _layer_spec(w_mq, l), mem, mem, _layer_spec(w_mo, l)],
        out_specs=tok(d),
        out_shape=jax.ShapeDtypeStruct((n, d), F32),
        compiler_params=_cparams("parallel"),
        name="mixer_out_xattn",
    )(x, ya, yb, w_out, g, w_mq, mk, mv, w_mo)


def _memkv_kernel(m_ref, g_ref, wk_ref, wv_ref, k_ref, v_ref):
    tm = m_ref.shape[0]
    m = _rms(m_ref[...], g_ref[...]).astype(BF16)
    for w_ref, o_ref in ((wk_ref, k_ref), (wv_ref, v_ref)):
        kv = _dot(m, w_ref[...])
        for hd in range(MEM_HEADS):
            for j in range(MEM_LANE_HALVES):
                col = hd * MEM_HD + j * 128
                o_ref[pl.ds(j * MEM_HEADS + hd, tm, stride=MEM_ROWS), :] = kv[:, col:col + 128]


def mem_kv(mem, l, g, w_mk, w_mv):
    n, d = mem.shape
    tm = min(TOKEN_TILE, n)
    rows = pl.BlockSpec((tm * MEM_ROWS, 128), lambda i: (i, 0))
    return pl.pallas_call(
        _memkv_kernel,
        grid=(n // tm,),
        in_specs=[pl.BlockSpec((tm, d), lambda i: (i, 0)), _layer_spec(g, l), _layer_spec(w_mk, l),
                  _layer_spec(w_mv, l)],
        out_specs=[rows, rows],
        out_shape=[jax.ShapeDtypeStruct((n * MEM_ROWS, 128), F32)] * 2,
        compiler_params=_cparams("parallel"),
        name="mem_kv",
    )(mem, g, w_mk, w_mv)


def _rot_half_cols(w):
    half = DR // 2
    shp = w.shape
    w = w.reshape(shp[:-1] + (shp[-1] // DR, 2, half))
    return jnp.concatenate([-w[..., 1:, :], w[..., :1, :]], axis=-2).reshape(shp)


def _mem_rows(a):
    lead = a.shape[:-3]
    a = a.reshape(lead + (N_MEM, MEM_HEADS, MEM_LANE_HALVES, 128))
    return jnp.swapaxes(a, -2, -3).reshape(lead + (N_MEM * MEM_ROWS, 128))


def _mem_heads(a):
    lead = a.shape[:-2]
    a = a.reshape(lead + (N_MEM, MEM_LANE_HALVES, MEM_HEADS, 128))
    return jnp.swapaxes(a, -2, -3).reshape(lead + (N_MEM, MEM_HEADS, MEM_HD))


def _rope_tables(pos, reps):
    half = DR // 2
    inv = ROPE_THETA ** (-jnp.arange(half, dtype=F32) / half)
    ang = pos.astype(F32)[:, None] * inv[None, :]
    cos = jnp.tile(jnp.cos(ang), (1, 2 * reps))
    sin = jnp.tile(jnp.sin(ang), (1, 2 * reps))
    return cos, sin


def _group_tables(pos, n_tokens):
    rows = max(pos.shape[0], min(TOKEN_TILE, n_tokens))
    pos = jnp.tile(pos, rows // pos.shape[0])
    return _rope_tables(pos, H_A) + _rope_tables(pos, 1)


def _layer_group(x, tabs, lw, l, b, t, past_ckv, past_kr, ls, wkv0, shift0, lm, mk, mv, *, tq, kb, final_norm):
    x = ffn_half(x, l, lw["ffn1_norm"], lw["ffn1_wg"], lw["ffn1_wu"], lw["ffn1_wd"], lw["final_norm"],
                 final_norm=False)
    q, ckv, kr, pb = mixer_proj(x, l, lw["mix_norm"], lw["w_in_ext"], lw["q_norm"], lw["w_uq_ext"], lw["w_uk"],
                                lw["kv_norm"], *tabs)
    ckv3, kr3 = ckv.reshape(b, t, KV_RANK), kr.reshape(b, t, DR)
    ya = mla_attention(q, ckv3, kr3, l, past_ckv, past_kr, lw["w_uv_t"], tq=tq, kb=kb)
    pb3 = pb.reshape(b, t, D_SHIFT)
    yb, wkv = wkv_group(pb3, l, ls, shift0, wkv0, *(lw[n] for n in (
        "shift_mu", "w0", "w_up", "a0", "a_up", "g_up", "k_k", "k_a", "r_k", "gn_gain", "gn_bias")))
    x = mixer_out_xattn(x, ya.reshape(b * t, H_A * DV), yb.reshape(b * t, D_B), l, lw["w_out"], lw["xattn_norm"],
                        lw["w_mq"], lm, mk, mv, lw["w_mo"], t_stream=t)
    x = ffn_half(x, l, lw["ffn2_norm"], lw["ffn2_wg"], lw["ffn2_wu"], lw["ffn2_wd"], lw["final_norm"],
                 final_norm=final_norm)
    return x, ckv3, kr3, wkv, pb3[:, -1:, :]


def kernel(x_prompt, x_sample, mem_prompt, cache_ckv, cache_krope, cache_mem_k, cache_mem_v, state_wkv, state_shift, ffn1_norm, ffn1_w_gate, ffn1_w_up, ffn1_w_down, mix_norm, w_in, q_norm, w_uq, kv_norm, w_uk, w_uv, shift_mu, w0, w_up, a0, a_up, g_up, k_k, k_a, r_k, gn_gain, gn_bias, w_out, xattn_norm, mem_kv_norm, w_mq, w_mk, w_mv, w_mo, ffn2_norm, ffn2_w_gate, ffn2_w_up, ffn2_w_down, final_norm):
    depth = w_in.shape[0]
    b_p, t_p, d = x_prompt.shape
    b_s, t_s, _ = x_sample.shape
    n_past = cache_ckv.shape[2]
    bf = lambda a: a.astype(BF16)
    vec = lambda a: a.reshape(depth, 1, -1)

    o = Q_RANK + KV_RANK
    w_kr = w_in[..., o:o + DR]
    w_in_ext = jnp.concatenate(
        [w_in[..., :o + DR], _rot_half_cols(w_kr), jnp.zeros(w_in.shape[:2] + (PROJ_HEAD - o - 2 * DR,), w_in.dtype),
         w_in[..., o + DR:]], axis=-1)
    uq = w_uq.reshape(depth, Q_RANK, H_A, DN + DR)
    uq_nope = uq[..., :DN].reshape(depth, Q_RANK, H_A * DN)
    uq_rope = uq[..., DN:].reshape(depth, Q_RANK, H_A * DR)
    w_uq_ext = jnp.concatenate([uq_nope, uq_rope, _rot_half_cols(uq_rope)], axis=-1)
    lw = dict(
        ffn1_norm=vec(ffn1_norm), ffn1_wg=bf(ffn1_w_gate), ffn1_wu=bf(ffn1_w_up), ffn1_wd=bf(ffn1_w_down),
        mix_norm=vec(mix_norm), w_in_ext=bf(w_in_ext), q_norm=vec(q_norm), w_uq_ext=bf(w_uq_ext),
        kv_norm=vec(kv_norm), w_uk=bf(w_uk), w_uv_t=bf(jnp.swapaxes(w_uv, -1, -2)), shift_mu=vec(shift_mu),
        w0=vec(w0), w_up=bf(w_up), a0=vec(a0), a_up=bf(a_up), g_up=bf(g_up), k_k=vec(k_k), k_a=vec(k_a),
        r_k=vec(r_k), gn_gain=vec(gn_gain), gn_bias=vec(gn_bias), w_out=bf(w_out), xattn_norm=vec(xattn_norm),
        w_mq=bf(w_mq), w_mo=bf(w_mo), ffn2_norm=vec(ffn2_norm), ffn2_wg=bf(ffn2_w_gate), ffn2_wu=bf(ffn2_w_up),
        ffn2_wd=bf(ffn2_w_down), final_norm=final_norm.reshape(1, 1, d))
    mem_norm, w_mk_b, w_mv_b = vec(mem_kv_norm), bf(w_mk), bf(w_mv)

    tabs_p = _group_tables(jnp.arange(t_p, dtype=jnp.int32), b_p * t_p)
    tabs_s = _group_tables(n_past + jnp.arange(t_s, dtype=jnp.int32), b_s * t_s)
    wkv_zero = jnp.zeros((1, b_p, H_B, N_B, N_B), x_prompt.dtype)
    shift_zero = jnp.zeros((1, b_p, 1, D_SHIFT), x_prompt.dtype)
    krope_t = jnp.swapaxes(cache_krope, -1, -2)
    mem_k_s, mem_v_s = _mem_rows(cache_mem_k), _mem_rows(cache_mem_v)
    kb_s = -(-(n_past + t_s) // (3 * 128)) * 128

    xp = x_prompt.reshape(b_p * t_p, d)
    xs = x_sample.reshape(b_s * t_s, d)
    mem2 = mem_prompt.reshape(b_p * N_MEM, d)
    outs_p = [[] for _ in range(6)]
    outs_s = [[] for _ in range(4)]
    for l in range(depth):
        last = l == depth - 1
        mk, mv = (a.reshape(1, b_p, N_MEM * MEM_ROWS, 128) for a in mem_kv(mem2, l, mem_norm, w_mk_b, w_mv_b))
        xp, c1, k1, s1, h1 = _layer_group(xp, tabs_p, lw, l, b_p, t_p, None, None, 0, wkv_zero, shift_zero, 0,
                                          mk, mv, tq=min(512, t_p), kb=512, final_norm=last)
        for acc, val in zip(outs_p, (c1, k1, mk[0], mv[0], s1, h1)):
            acc.append(val)
        xs, c2, k2, s2, h2 = _layer_group(xs, tabs_s, lw, l, b_s, t_s, cache_ckv, krope_t, l, state_wkv,
                                          state_shift, l, mem_k_s, mem_v_s, tq=t_s, kb=kb_s, final_norm=last)
        for acc, val in zip(outs_s, (c2, k2, s2, h2)):
            acc.append(val)
    stack = lambda seq: jnp.stack(seq)
    return (xp.reshape(b_p, t_p, d), xs.reshape(b_s, t_s, d),
            stack(outs_p[0]), stack(outs_p[1]), _mem_heads(stack(outs_p[2])), _mem_heads(stack(outs_p[3])),
            stack(outs_p[4]), stack(outs_p[5]),
            stack(outs_s[0]), stack(outs_s[1]), stack(outs_s[2]), stack(outs_s[3]))
```

```python
import functools
import math

import jax
import jax.numpy as jnp
from jax import lax
from jax.experimental import pallas as pl
from jax.experimental.pallas import tpu as pltpu

F32 = jnp.float32
BF16 = jnp.bfloat16

D_MODEL = 1024
CHUNK = 64
H_A, DN, DR, DV = 8, 64, 32, 64
Q_RANK, KV_RANK = 256, 128
DQK = KV_RANK + DR
ROPE_THETA = 10000.0
MLA_SCALE = 1.0 / math.sqrt(DN + DR)
LOG2E = math.log2(math.e)
H_B, N_B = 8, 64
D_B = H_B * N_B
W_RANK, A_RANK, G_RANK = 64, 64, 128
D_SHIFT = 3 * D_B + W_RANK + A_RANK + G_RANK
N_MEM, MEM_HEADS = 256, 4
MEM_HD = D_MODEL // MEM_HEADS
MEM_LANE_HALVES = MEM_HD // 128
MEM_ROWS = MEM_LANE_HALVES * MEM_HEADS
D_FF = 2816
EPS = 1e-6
GN_EPS = 64e-5
NEG_INF = -1e30

PROJ_HEAD = 512
D_PROJ = PROJ_HEAD + D_SHIFT

TOKEN_TILE = 512
WKV_STREAMS = 4
VMEM_LIMIT = 56 * 1024 * 1024


def _cparams(*sem):
    return pltpu.CompilerParams(dimension_semantics=sem, vmem_limit_bytes=VMEM_LIMIT)


def _layer_spec(arr, l):
    nd = arr.ndim - 1
    return pl.BlockSpec((None,) + arr.shape[1:], lambda *_: (l,) + (0,) * nd, pipeline_mode=pl.Buffered(1))


def _rms(x, g):
    return x * lax.rsqrt(jnp.mean(x * x, axis=-1, keepdims=True) + EPS) * g


def _dot(a, b):
    return jnp.dot(a, b, preferred_element_type=F32)


def _dot_nt(a, b):
    return lax.dot_general(a, b, (((1,), (1,)), ((), ())), preferred_element_type=F32)


def _bmm(a, b):
    return jnp.einsum("gmk,gkn->gmn", a, b, preferred_element_type=F32)


def _bmm_nt(a, b):
    return jnp.einsum("gmk,gnk->gmn", a, b, preferred_element_type=F32)


def _ffn_kernel(x_ref, g_ref, wg_ref, wu_ref, wd_ref, gf_ref, o_ref, *, final_norm):
    x = x_ref[...]
    h = _rms(x, g_ref[...]).astype(BF16)
    gate = _dot(h, wg_ref[...])
    up = _dot(h, wu_ref[...])
    act = (gate * jax.nn.sigmoid(gate) * up).astype(BF16)
    y = x + 0.5 * _dot(act, wd_ref[...])
    if final_norm:
        y = _rms(y, gf_ref[...])
    o_ref[...] = y


def ffn_half(x, l, g, wg, wu, wd, gf, *, final_norm):
    n, d = x.shape
    tm = min(TOKEN_TILE, n)
    return pl.pallas_call(
        functools.partial(_ffn_kernel, final_norm=final_norm),
        grid=(n // tm,),
        in_specs=[pl.BlockSpec((tm, d), lambda i: (i, 0)), _layer_spec(g, l), _layer_spec(wg, l),
                  _layer_spec(wu, l), _layer_spec(wd, l), _layer_spec(gf, 0)],
        out_specs=pl.BlockSpec((tm, d), lambda i: (i, 0)),
        out_shape=jax.ShapeDtypeStruct((n, d), F32),
        compiler_params=_cparams("parallel"),
        name="ffn_half",
    )(x, g, wg, wu, wd, gf)


def _proj_kernel(x_ref, g_ref, win_ref, qn_ref, wuq_ref, wuk_ref, kvn_ref, cq_ref, sq_ref, ck_ref, sk_ref,
                 q_ref, ckv_ref, kr_ref, pb_ref):
    h = _rms(x_ref[...], g_ref[...]).astype(BF16)
    proj = _dot(h, win_ref[...])
    c_q = proj[:, :Q_RANK]
    c_kv = proj[:, Q_RANK:Q_RANK + KV_RANK]
    o = Q_RANK + KV_RANK
    k_r, k_rot = proj[:, o:o + DR], proj[:, o + DR:o + 2 * DR]
    ckv_ref[...] = _rms(c_kv, kvn_ref[...])
    kr_t = (k_r * ck_ref[...] + k_rot * sk_ref[...]).T
    tw = kr_ref.shape[2]
    for s in range(kr_ref.shape[0]):
        kr_ref[s] = kr_t[:, s * tw:(s + 1) * tw]
    pb_ref[...] = proj[:, PROJ_HEAD:]
    qf = _dot(_rms(c_q, qn_ref[...]).astype(BF16), wuq_ref[...])
    nq = H_A * DN
    q_rope = (qf[:, nq:nq + H_A * DR] * cq_ref[...] + qf[:, nq + H_A * DR:] * sq_ref[...]) * (MLA_SCALE * LOG2E)
    for hd in range(H_A):
        q_lat = _dot_nt(qf[:, hd * DN:(hd + 1) * DN].astype(BF16), wuk_ref[hd]) * (MLA_SCALE * LOG2E)
        q_ref[hd, :, :KV_RANK] = q_lat.astype(BF16)
        q_ref[hd, :, KV_RANK:] = q_rope[:, hd * DR:(hd + 1) * DR].astype(BF16)


def mixer_proj(x, l, g, w_in_ext, q_norm, w_uq_ext, w_uk, kv_norm, cos_q, sin_q, cos_k, sin_k, *, t_stream):
    n, d = x.shape
    tm = min(TOKEN_TILE, n)
    n_tab = cos_q.shape[0] // tm
    spt = max(1, tm // t_stream)
    tps = max(1, t_stream // tm)
    tok = lambda w: pl.BlockSpec((tm, w), lambda i: (i, 0))
    tab = lambda w: pl.BlockSpec((tm, w), lambda i: (i % n_tab, 0))
    kr_spec = pl.BlockSpec((spt, DR, tm // spt), lambda i: (i // tps, 0, i % tps))
    return pl.pallas_call(
        _proj_kernel,
        grid=(n // tm,),
        in_specs=[tok(d), _layer_spec(g, l), _layer_spec(w_in_ext, l), _layer_spec(q_norm, l),
                  _layer_spec(w_uq_ext, l), _layer_spec(w_uk, l), _layer_spec(kv_norm, l),
                  tab(H_A * DR), tab(H_A * DR), tab(DR), tab(DR)],
        out_specs=[pl.BlockSpec((H_A, tm, DQK), lambda i: (0, i, 0)), tok(KV_RANK), kr_spec, tok(D_SHIFT)],
        out_shape=[jax.ShapeDtypeStruct((H_A, n, DQK), BF16), jax.ShapeDtypeStruct((n, KV_RANK), F32),
                   jax.ShapeDtypeStruct((n // t_stream, DR, t_stream), F32),
                   jax.ShapeDtypeStruct((n, D_SHIFT), F32)],
        compiler_params=_cparams("parallel"),
        name="mixer_proj",
    )(x, g, w_in_ext, q_norm, w_uq_ext, w_uk, kv_norm, cos_q, sin_q, cos_k, sin_k)


def _mla_kernel(*refs, tq, kb, n_past, t_new, has_past):
    if has_past:
        q_ref, ckv_ref, kr_ref, pckv_ref, pkr_ref, wuvt_ref, o_ref, kcat, vt = refs
    else:
        q_ref, ckv_ref, kr_ref, wuvt_ref, o_ref, kcat, vt = refs
    i = pl.program_id(1)
    t_all = n_past + t_new
    t_pad = kcat.shape[0]

    @pl.when(i == 0)
    def _():
        if t_pad > t_all:
            kcat[t_all:, :] = jnp.zeros((t_pad - t_all, DQK), BF16)
            vt[:, t_all:] = jnp.zeros((KV_RANK, t_pad - t_all), BF16)
        pieces = [(0, pckv_ref, pkr_ref, n_past)] if has_past else []
        for base, c_ref, r_ref, n in pieces + [(n_past, ckv_ref, kr_ref, t_new)]:
            kcat[base:base + n, :KV_RANK] = c_ref[0].astype(BF16)
            step_t = min(n, 512)
            for o in range(0, n, step_t):
                vt[:, base + o:base + o + step_t] = c_ref[0, o:o + step_t, :].T.astype(BF16)
                kcat[base + o:base + o + step_t, KV_RANK:] = r_ref[0, :, o:o + step_t].T.astype(BF16)

    rows = H_A * tq
    q = q_ref[...].reshape(rows, DQK)
    t_in = lax.broadcasted_iota(jnp.int32, (1, rows), 1) & (tq - 1)
    limit = ((n_past + i * tq + t_in) // CHUNK + 1) * CHUNK
    n_blocks = (n_past + (i + 1) * tq + kb - 1) // kb

    def scores(j):
        return _dot_nt(kcat[pl.ds(pl.multiple_of(j * kb, kb), kb), :], q)

    def update(j, carry, st, masked):
        m, l, acc = carry
        if masked:
            k_pos = j * kb + lax.broadcasted_iota(jnp.int32, (kb, 1), 0)
            st = jnp.where(k_pos < limit, st, NEG_INF)
        m_new = jnp.maximum(m, jnp.max(st, axis=0, keepdims=True))
        alpha = jnp.exp2(m - m_new)
        p = jnp.exp2(st - m_new)
        l = alpha * l + jnp.sum(p, axis=0, keepdims=True)
        acc = alpha * acc + _dot(vt[:, pl.ds(pl.multiple_of(j * kb, kb), kb)], p.astype(BF16))
        return m_new, l, acc

    def pair(jj, carry):
        st0, st1 = scores(2 * jj), scores(2 * jj + 1)
        return update(2 * jj + 1, update(2 * jj, carry, st0, False), st1, False)

    def single(j, carry):
        return update(j, carry, scores(j), False)

    m0 = jnp.full((1, rows), NEG_INF, F32)
    l0 = jnp.zeros((1, rows), F32)
    a0 = jnp.zeros((KV_RANK, rows), F32)
    n_pairs = (n_blocks - 1) // 2
    carry = lax.fori_loop(0, n_pairs, pair, (m0, l0, a0))
    carry = lax.fori_loop(2 * n_pairs, n_blocks - 1, single, carry)
    _, l, acc = update(n_blocks - 1, carry, scores(n_blocks - 1), True)
    o_lat = (acc / l).astype(BF16)
    outs = [_dot(wuvt_ref[hd], o_lat[:, hd * tq:(hd + 1) * tq]) for hd in range(H_A)]
    o_ref[0] = jnp.concatenate(outs, axis=0).T.astype(o_ref.dtype)


def mla_attention(q, ckv, kr, l, past_ckv, past_kr, w_uv_t, *, tq, kb):
    b, t, _ = ckv.shape
    has_past = past_ckv is not None
    n_past = past_ckv.shape[2] if has_past else 0
    t_pad = -(-(n_past + t) // kb) * kb
    nq = t // tq
    for i in range(nq):
        first_limit = ((n_past + i * tq) // CHUNK + 1) * CHUNK
        assert first_limit // kb >= -(-(n_past + (i + 1) * tq) // kb) - 1, (tq, kb, n_past, i)
    stream = lambda n, w: pl.BlockSpec((1, n, w), lambda bi, i: (bi, 0, 0))
    past_c = pl.BlockSpec((None, 1, n_past, KV_RANK), lambda bi, i: (l, bi, 0, 0))
    past_r = pl.BlockSpec((None, 1, DR, n_past), lambda bi, i: (l, bi, 0, 0))
    in_specs = [pl.BlockSpec((H_A, tq, DQK), lambda bi, i: (0, bi * nq + i, 0)), stream(t, KV_RANK), stream(DR, t)]
    args = [q, ckv, kr]
    if has_past:
        in_specs += [past_c, past_r]
        args += [past_ckv, past_kr]
    in_specs.append(_layer_spec(w_uv_t, l))
    args.append(w_uv_t)
    return pl.pallas_call(
        functools.partial(_mla_kernel, tq=tq, kb=kb, n_past=n_past, t_new=t, has_past=has_past),
        grid=(b, nq),
        in_specs=in_specs,
        out_specs=pl.BlockSpec((1, tq, H_A * DV), lambda bi, i: (bi, i, 0)),
        out_shape=jax.ShapeDtypeStruct((b, t, H_A * DV), BF16),
        scratch_shapes=[pltpu.VMEM((t_pad, DQK), BF16), pltpu.VMEM((KV_RANK, t_pad), BF16)],
        compiler_params=_cparams("parallel", "arbitrary"),
        name="mla_attention",
    )(*args)


def _split3(x):
    hi = x.astype(BF16)
    r1 = x - hi.astype(F32)
    mid = r1.astype(BF16)
    lo = (r1 - mid.astype(F32)).astype(BF16)
    return hi, mid, lo


def _wkv_kernel(pb_ref, sh0_ref, s0_ref, mu_ref, w0_ref, wup_ref, a0_ref, aup_ref, gup_ref, kk_ref, ka_ref,
                rk_ref, gng_ref, gnb_ref, seg_ref, y_ref, so_ref, st_scr, prev_scr, *, sb):
    c = pl.program_id(1)
    nb = sb * H_B
    n = sb * CHUNK

    @pl.when(c == 0)
    def _():
        st_scr[...] = jnp.swapaxes(s0_ref[...].reshape(nb, N_B, N_B), 1, 2)
        prev_scr[...] = sh0_ref[...]

    row = lax.broadcasted_iota(jnp.int32, (CHUNK, 1), 0)
    parts = []
    for s in range(sb):
        pb = pb_ref[s]
        prev = jnp.where(row == 0, prev_scr[s], pltpu.roll(pb, shift=1, axis=0))
        prev_scr[s] = pb[CHUNK - 1:, :]
        parts.append(pb + mu_ref[...] * (prev - pb))
    xs = jnp.concatenate(parts, axis=0)
    r = xs[:, :D_B]
    k = xs[:, D_B:2 * D_B]
    v = xs[:, 2 * D_B:3 * D_B]
    o = 3 * D_B
    xw, xa, xg = xs[:, o:o + W_RANK], xs[:, o + W_RANK:o + W_RANK + A_RANK], xs[:, o + W_RANK + A_RANK:]
    z = w0_ref[...] + _dot(jnp.tanh(xw).astype(BF16), wup_ref[...])
    logw = -math.exp(-0.5) * jax.nn.sigmoid(z)
    gate = jax.nn.sigmoid(a0_ref[...] + _dot(xa.astype(BF16), aup_ref[...]))
    g = _dot(jax.nn.sigmoid(xg).astype(BF16), gup_ref[...])
    kkf = k * kk_ref[...]
    k2 = k * (1.0 + (gate - 1.0) * ka_ref[...])
    ri = lax.broadcasted_iota(jnp.int32, (n, 1), 0)
    ci = lax.broadcasted_iota(jnp.int32, (1, n), 1)
    same_stream = jnp.right_shift(ri, 6) == jnp.right_shift(ci, 6)
    ones_tri = jnp.where(ri >= ci, jnp.where(same_stream, 1.0, 0.0), 0.0).astype(BF16)
    cs = sum(_dot(ones_tri, piece) for piece in _split3(logw))
    p_in = jnp.exp(cs)
    p_prev = jnp.exp(cs - logw)
    p_inv = jnp.exp(-cs)

    ones_blk = seg_ref[...]

    def seg_sum(x):
        hi = x.astype(BF16)
        lo = (x - hi.astype(F32)).astype(BF16)
        return _dot(hi, ones_blk) + _dot(lo, ones_blk)

    def heads3(x):
        return jnp.stack([x[s * CHUNK:(s + 1) * CHUNK, h * N_B:(h + 1) * N_B]
                          for s in range(sb) for h in range(H_B)], axis=0)

    kk = kkf * lax.rsqrt(seg_sum(kkf * kkf) + 1e-12)
    bonus = seg_sum(r * k2 * rk_ref[...]) * v
    at32, rt32 = heads3(-kk * p_prev), heads3(r * p_in)
    kt32, bt32 = heads3(k2 * p_inv), heads3(kk * gate * p_inv)
    v3 = heads3(v)
    pend3 = jnp.stack([p_in[(s + 1) * CHUNK - 1:(s + 1) * CHUNK, h * N_B:(h + 1) * N_B]
                       for s in range(sb) for h in range(H_B)], axis=0)
    at, rt = at32.astype(BF16), rt32.astype(BF16)
    bk = jnp.concatenate([bt32, kt32], axis=1).astype(BF16)
    bhkh_t = jnp.swapaxes(jnp.concatenate([bt32 * pend3, kt32 * pend3], axis=1), 1, 2).astype(BF16)
    vb = v3.astype(BF16)

    tt = lax.broadcasted_iota(jnp.int32, (1, CHUNK, 2 * CHUNK), 1)
    ss = lax.broadcasted_iota(jnp.int32, (1, CHUNK, 2 * CHUNK), 2) & (CHUNK - 1)
    a_top = jnp.where(tt > ss, _bmm_nt(at, bk), 0.0)
    a_bot = jnp.where(tt >= ss, _bmm_nt(rt, bk), 0.0).astype(BF16)
    zero_v = jnp.concatenate([jnp.zeros((nb, CHUNK, N_B), BF16), vb], axis=1)
    zz = jnp.concatenate([at32, _bmm(a_top.astype(BF16), zero_v)], axis=-1)
    x = a_top[:, :, :CHUNK]
    for lvl in range(6):
        xb = x.astype(BF16)
        if lvl < 5:
            res = _bmm(xb, jnp.concatenate([zz.astype(BF16), xb], axis=-1))
            zz = zz + res[:, :, :2 * N_B]
            x = res[:, :, 2 * N_B:]
        else:
            zz = zz + _bmm(xb, zz.astype(BF16))
    w_m, u0 = zz[:, :, :N_B].astype(BF16), zz[:, :, N_B:].astype(BF16)
    res2 = _bmm(jnp.concatenate([a_bot[:, :, :CHUNK], bhkh_t[:, :, :CHUNK]], axis=1), w_m)
    q_hat = rt32 + res2[:, :CHUNK]
    eye = lax.broadcasted_iota(jnp.int32, (1, N_B, N_B), 1) == lax.broadcasted_iota(jnp.int32, (1, N_B, N_B), 2)
    m_t = jnp.where(eye, pend3, 0.0) + res2[:, CHUNK:]
    lhs = jnp.concatenate([jnp.concatenate([a_bot, q_hat.astype(BF16)], axis=-1),
                           jnp.concatenate([bhkh_t, m_t.astype(BF16)], axis=-1)], axis=1)
    rhs = jnp.concatenate([u0, vb, st_scr[...].astype(BF16)], axis=1)
    res3 = _bmm(lhs, rhs)
    y3 = res3[:, :CHUNK]
    st_scr[...] = res3[:, CHUNK:]

    y = jnp.concatenate([jnp.concatenate([y3[s * H_B + h] for h in range(H_B)], axis=-1) for s in range(sb)], axis=0)
    d = y - seg_sum(y) * (1.0 / N_B)
    var = seg_sum(d * d) * (1.0 / N_B)
    y = (d * lax.rsqrt(var + GN_EPS) * gng_ref[...] + gnb_ref[...] + bonus) * g
    for s in range(sb):
        y_ref[s] = y[s * CHUNK:(s + 1) * CHUNK].astype(y_ref.dtype)

    @pl.when(c == pl.num_programs(1) - 1)
    def _():
        so_ref[...] = jnp.swapaxes(st_scr[...], 1, 2).reshape(sb, H_B, N_B, N_B)


def wkv_group(pb, l, ls, shift0, s0, mu, w0, w_up, a0, a_up, g_up, k_k, k_a, r_k, gn_g, gn_b):
    b, t, _ = pb.shape
    sb = min(WKV_STREAMS, b)
    small = [mu, w0, w_up, a0, a_up, g_up, k_k, k_a, r_k, gn_g, gn_b]
    head_of = jnp.arange(D_B, dtype=jnp.int32) // N_B
    same_head = (head_of[:, None] == head_of[None, :]).astype(BF16)
    return pl.pallas_call(
        functools.partial(_wkv_kernel, sb=sb),
        grid=(b // sb, t // CHUNK),
        in_specs=[pl.BlockSpec((sb, CHUNK, D_SHIFT), lambda bi, c: (bi, c, 0)),
                  pl.BlockSpec((None, sb, 1, D_SHIFT), lambda bi, c: (ls, bi, 0, 0)),
                  pl.BlockSpec((None, sb, H_B, N_B, N_B), lambda bi, c: (ls, bi, 0, 0, 0))]
        + [_layer_spec(a, l) for a in small]
        + [pl.BlockSpec((D_B, D_B), lambda bi, c: (0, 0), pipeline_mode=pl.Buffered(1))],
        out_specs=[pl.BlockSpec((sb, CHUNK, D_B), lambda bi, c: (bi, c, 0)),
                   pl.BlockSpec((sb, H_B, N_B, N_B), lambda bi, c: (bi, 0, 0, 0))],
        out_shape=[jax.ShapeDtypeStruct((b, t, D_B), BF16), jax.ShapeDtypeStruct((b, H_B, N_B, N_B), F32)],
        scratch_shapes=[pltpu.VMEM((sb * H_B, N_B, N_B), F32), pltpu.VMEM((sb, 1, D_SHIFT), F32)],
        compiler_params=_cparams("parallel", "arbitrary"),
        name="wkv_group",
    )(pb, shift0, s0, *small, same_head)


def _out_kernel(x_ref, ya_ref, yb_ref, wo_ref, g_ref, wq_ref, mk_ref, mv_ref, wmo_ref, o_ref, *, spt):
    x = x_ref[...] + _dot(ya_ref[...], wo_ref[:H_A * DV, :]) + _dot(yb_ref[...], wo_ref[H_A * DV:, :])
    q = _dot(_rms(x, g_ref[...]).astype(BF16), wq_ref[...])
    ts = x.shape[0] // spt
    halves = lambda hd: [pl.ds(j * MEM_HEADS + hd, N_MEM, stride=MEM_ROWS) for j in range(MEM_LANE_HALVES)]
    pairs = [(s, hd) for s in range(spt) for hd in range(MEM_HEADS)]
    scores = []
    for s, hd in pairs:
        qh = q[s * ts:(s + 1) * ts, hd * MEM_HD:(hd + 1) * MEM_HD].astype(BF16)
        scores.append(sum(_dot_nt(qh[:, j * 128:(j + 1) * 128], mk_ref[s, rws, :].astype(BF16))
                          for j, rws in enumerate(halves(hd))) / math.sqrt(MEM_HD))
    probs = []
    for sc in scores:
        sc = jnp.exp(sc - jnp.max(sc, axis=-1, keepdims=True))
        probs.append((sc / jnp.sum(sc, axis=-1, keepdims=True)).astype(BF16))
    outs = {}
    for (s, hd), p in zip(pairs, probs):
        outs[s, hd] = [_dot(p, mv_ref[s, rws, :].astype(BF16)) for rws in halves(hd)]
    rows = []
    for s in range(spt):
        rows.append(jnp.concatenate([o for hd in range(MEM_HEADS) for o in outs[s, hd]], axis=-1))
    att = rows[0] if spt == 1 else jnp.concatenate(rows, axis=0)
    o_ref[...] = x + _dot(att.astype(BF16), wmo_ref[...])


def mixer_out_xattn(x, ya, yb, l, w_out, g, w_mq, lm, mk, mv, w_mo, *, t_stream):
    n, d = x.shape
    tm = min(2 * TOKEN_TILE if t_stream >= 2 * TOKEN_TILE else TOKEN_TILE, n)
    spt = max(1, tm // t_stream)
    tps = max(1, t_stream // tm)
    tok = lambda w: pl.BlockSpec((tm, w), lambda i: (i, 0))
    mem = pl.BlockSpec((None, spt, N_MEM * MEM_ROWS, 128), lambda i: (lm, i // tps, 0, 0))
    return pl.pallas_call(
        functools.partial(_out_kernel, spt=spt),
        grid=(n // tm,),
        in_specs=[tok(d), tok(H_A * DV), tok(D_B), _layer_spec(w_out, l), _layer_spec(g, l),
                  _layer_spec(w_mq, l), mem, mem, _layer_spec(w_mo, l)],
        out_specs=tok(d),
        out_shape=jax.ShapeDtypeStruct((n, d), F32),
        compiler_params=_cparams("parallel"),
        name="mixer_out_xattn",
    )(x, ya, yb, w_out, g, w_mq, mk, mv, w_mo)


def _memkv_kernel(m_ref, g_ref, wk_ref, wv_ref, k_ref, v_ref):
    tm = m_ref.shape[0]
    m = _rms(m_ref[...], g_ref[...]).astype(BF16)
    for w_ref, o_ref in ((wk_ref, k_ref), (wv_ref, v_ref)):
        kv = _dot(m, w_ref[...])
        for hd in range(MEM_HEADS):
            for j in range(MEM_LANE_HALVES):
                col = hd * MEM_HD + j * 128
                o_ref[pl.ds(j * MEM_HEADS + hd, tm, stride=MEM_ROWS), :] = kv[:, col:col + 128]


def mem_kv(mem, l, g, w_mk, w_mv):
    n, d = mem.shape
    tm = min(TOKEN_TILE, n)
    rows = pl.BlockSpec((tm * MEM_ROWS, 128), lambda i: (i, 0))
    return pl.pallas_call(
        _memkv_kernel,
        grid=(n // tm,),
        in_specs=[pl.BlockSpec((tm, d), lambda i: (i, 0)), _layer_spec(g, l), _layer_spec(w_mk, l),
                  _layer_spec(w_mv, l)],
        out_specs=[rows, rows],
        out_shape=[jax.ShapeDtypeStruct((n * MEM_ROWS, 128), F32)] * 2,
        compiler_params=_cparams("parallel"),
        name="mem_kv",
    )(mem, g, w_mk, w_mv)


def _rot_half_cols(w):
    half = DR // 2
    shp = w.shape
    w = w.reshape(shp[:-1] + (shp[-1] // DR, 2, half))
    return jnp.concatenate([-w[..., 1:, :], w[..., :1, :]], axis=-2).reshape(shp)


def _mem_rows(a):
    lead = a.shape[:-3]
    a = a.reshape(lead + (N_MEM, MEM_HEADS, MEM_LANE_HALVES, 128))
    return jnp.swapaxes(a, -2, -3).reshape(lead + (N_MEM * MEM_ROWS, 128))


def _mem_heads(a):
    lead = a.shape[:-2]
    a = a.reshape(lead + (N_MEM, MEM_LANE_HALVES, MEM_HEADS, 128))
    return jnp.swapaxes(a, -2, -3).reshape(lead + (N_MEM, MEM_HEADS, MEM_HD))


def _rope_tables(pos, reps):
    half = DR // 2
    inv = ROPE_THETA ** (-jnp.arange(half, dtype=F32) / half)
    ang = pos.astype(F32)[:, None] * inv[None, :]
    cos = jnp.tile(jnp.cos(ang), (1, 2 * reps))
    sin = jnp.tile(jnp.sin(ang), (1, 2 * reps))
    return cos, sin


def _group_tables(pos, n_tokens):
    rows = max(pos.shape[0], min(TOKEN_TILE, n_tokens))
    pos = jnp.tile(pos, rows // pos.shape[0])
    return _rope_tables(pos, H_A) + _rope_tables(pos, 1)


def _layer_group(x, tabs, lw, l, b, t, past_ckv, past_kr, ls, wkv0, shift0, lm, mk, mv, *, tq, kb, final_norm):
    x = ffn_half(x, l, lw["ffn1_norm"], lw["ffn1_wg"], lw["ffn1_wu"], lw["ffn1_wd"], lw["final_norm"],
                 final_norm=False)
    q, ckv, kr, pb = mixer_proj(x, l, lw["mix_norm"], lw["w_in_ext"], lw["q_norm"], lw["w_uq_ext"], lw["w_uk"],
                                lw["kv_norm"], *tabs, t_stream=t)
    ckv3 = ckv.reshape(b, t, KV_RANK)
    ya = mla_attention(q, ckv3, kr, l, past_ckv, past_kr, lw["w_uv_t"], tq=tq, kb=kb)
    pb3 = pb.reshape(b, t, D_SHIFT)
    yb, wkv = wkv_group(pb3, l, ls, shift0, wkv0, *(lw[n] for n in (
        "shift_mu", "w0", "w_up", "a0", "a_up", "g_up", "k_k", "k_a", "r_k", "gn_gain", "gn_bias")))
    x = mixer_out_xattn(x, ya.reshape(b * t, H_A * DV), yb.reshape(b * t, D_B), l, lw["w_out"], lw["xattn_norm"],
                        lw["w_mq"], lm, mk, mv, lw["w_mo"], t_stream=t)
    x = ffn_half(x, l, lw["ffn2_norm"], lw["ffn2_wg"], lw["ffn2_wu"], lw["ffn2_wd"], lw["final_norm"],
                 final_norm=final_norm)
    return x, ckv3, kr, wkv, pb3[:, -1:, :]


def kernel(x_prompt, x_sample, mem_prompt, cache_ckv, cache_krope, cache_mem_k, cache_mem_v, state_wkv, state_shift, ffn1_norm, ffn1_w_gate, ffn1_w_up, ffn1_w_down, mix_norm, w_in, q_norm, w_uq, kv_norm, w_uk, w_uv, shift_mu, w0, w_up, a0, a_up, g_up, k_k, k_a, r_k, gn_gain, gn_bias, w_out, xattn_norm, mem_kv_norm, w_mq, w_mk, w_mv, w_mo, ffn2_norm, ffn2_w_gate, ffn2_w_up, ffn2_w_down, final_norm):
    depth = w_in.shape[0]
    b_p, t_p, d = x_prompt.shape
    b_s, t_s, _ = x_sample.shape
    n_past = cache_ckv.shape[2]
    bf = lambda a: a.astype(BF16)
    vec = lambda a: a.reshape(depth, 1, -1)

    o = Q_RANK + KV_RANK
    w_kr = w_in[..., o:o + DR]
    w_in_ext = jnp.concatenate(
        [w_in[..., :o + DR], _rot_half_cols(w_kr), jnp.zeros(w_in.shape[:2] + (PROJ_HEAD - o - 2 * DR,), w_in.dtype),
         w_in[..., o + DR:]], axis=-1)
    uq = w_uq.reshape(depth, Q_RANK, H_A, DN + DR)
    uq_nope = uq[..., :DN].reshape(depth, Q_RANK, H_A * DN)
    uq_rope = uq[..., DN:].reshape(depth, Q_RANK, H_A * DR)
    w_uq_ext = jnp.concatenate([uq_nope, uq_rope, _rot_half_cols(uq_rope)], axis=-1)
    lw = dict(
        ffn1_norm=vec(ffn1_norm), ffn1_wg=bf(ffn1_w_gate), ffn1_wu=bf(ffn1_w_up), ffn1_wd=bf(ffn1_w_down),
        mix_norm=vec(mix_norm), w_in_ext=bf(w_in_ext), q_norm=vec(q_norm), w_uq_ext=bf(w_uq_ext),
        kv_norm=vec(kv_norm), w_uk=bf(w_uk), w_uv_t=bf(jnp.swapaxes(w_uv, -1, -2)), shift_mu=vec(shift_mu),
        w0=vec(w0), w_up=bf(w_up), a0=vec(a0), a_up=bf(a_up), g_up=bf(g_up), k_k=vec(k_k), k_a=vec(k_a),
        r_k=vec(r_k), gn_gain=vec(gn_gain), gn_bias=vec(gn_bias), w_out=bf(w_out), xattn_norm=vec(xattn_norm),
        w_mq=bf(w_mq), w_mo=bf(w_mo), ffn2_norm=vec(ffn2_norm), ffn2_wg=bf(ffn2_w_gate), ffn2_wu=bf(ffn2_w_up),
        ffn2_wd=bf(ffn2_w_down), final_norm=final_norm.reshape(1, 1, d))
    mem_norm, w_mk_b, w_mv_b = vec(mem_kv_norm), bf(w_mk), bf(w_mv)

    tabs_p = _group_tables(jnp.arange(t_p, dtype=jnp.int32), b_p * t_p)
    tabs_s = _group_tables(n_past + jnp.arange(t_s, dtype=jnp.int32), b_s * t_s)
    wkv_zero = jnp.zeros((1, b_p, H_B, N_B, N_B), x_prompt.dtype)
    shift_zero = jnp.zeros((1, b_p, 1, D_SHIFT), x_prompt.dtype)
    krope_t = jnp.swapaxes(cache_krope, -1, -2)
    mem_k_s, mem_v_s = _mem_rows(cache_mem_k), _mem_rows(cache_mem_v)
    kb_s = -(-(n_past + t_s) // (3 * 128)) * 128

    xp = x_prompt.reshape(b_p * t_p, d)
    xs = x_sample.reshape(b_s * t_s, d)
    mem2 = mem_prompt.reshape(b_p * N_MEM, d)
    outs_p = [[] for _ in range(6)]
    outs_s = [[] for _ in range(4)]
    for l in range(depth):
        last = l == depth - 1
        mk, mv = (a.reshape(1, b_p, N_MEM * MEM_ROWS, 128) for a in mem_kv(mem2, l, mem_norm, w_mk_b, w_mv_b))
        xp, c1, k1, s1, h1 = _layer_group(xp, tabs_p, lw, l, b_p, t_p, None, None, 0, wkv_zero, shift_zero, 0,
                                          mk, mv, tq=min(512, t_p), kb=512, final_norm=last)
        for acc, val in zip(outs_p, (c1, k1, mk[0], mv[0], s1, h1)):
            acc.append(val)
        xs, c2, k2, s2, h2 = _layer_group(xs, tabs_s, lw, l, b_s, t_s, cache_ckv, krope_t, l, state_wkv,
                                          state_shift, l, mem_k_s, mem_v_s, tq=t_s, kb=kb_s, final_norm=last)
        for acc, val in zip(outs_s, (c2, k2, s2, h2)):
            acc.append(val)
    stack = lambda seq: jnp.stack(seq)
    rope_keys = lambda seq: jnp.swapaxes(jnp.stack(seq), -1, -2)
    return (xp.reshape(b_p, t_p, d), xs.reshape(b_s, t_s, d),
            stack(outs_p[0]), rope_keys(outs_p[1]), _mem_heads(stack(outs_p[2])), _mem_heads(stack(outs_p[3])),
            stack(outs_p[4]), stack(outs_p[5]),
            stack(outs_s[0]), rope_keys(outs_s[1]), stack(outs_s[2]), stack(outs_s[3]))
```

```python
import functools
import math

import jax
import jax.numpy as jnp
from jax import lax
from jax.experimental import pallas as pl
from jax.experimental.pallas import tpu as pltpu

F32 = jnp.float32
BF16 = jnp.bfloat16

D_MODEL = 1024
CHUNK = 64
H_A, DN, DR, DV = 8, 64, 32, 64
Q_RANK, KV_RANK = 256, 128
DQK = KV_RANK + DR
ROPE_THETA = 10000.0
MLA_SCALE = 1.0 / math.sqrt(DN + DR)
LOG2E = math.log2(math.e)
H_B, N_B = 8, 64
D_B = H_B * N_B
W_RANK, A_RANK, G_RANK = 64, 64, 128
D_SHIFT = 3 * D_B + W_RANK + A_RANK + G_RANK
N_MEM, MEM_HEADS = 256, 4
MEM_HD = D_MODEL // MEM_HEADS
MEM_LANE_HALVES = MEM_HD // 128
MEM_ROWS = MEM_LANE_HALVES * MEM_HEADS
D_FF = 2816
EPS = 1e-6
GN_EPS = 64e-5
NEG_INF = -1e30

PROJ_HEAD = 512
D_PROJ = PROJ_HEAD + D_SHIFT

TOKEN_TILE = 512
FUSED_TILE = 256
WKV_STREAMS = 4
VMEM_LIMIT = 56 * 1024 * 1024


def _cparams(*sem):
    return pltpu.CompilerParams(dimension_semantics=sem, vmem_limit_bytes=VMEM_LIMIT)


def _layer_spec(arr, l):
    nd = arr.ndim - 1
    return pl.BlockSpec((None,) + arr.shape[1:], lambda *_: (l,) + (0,) * nd, pipeline_mode=pl.Buffered(1))


def _rms(x, g):
    return x * lax.rsqrt(jnp.mean(x * x, axis=-1, keepdims=True) + EPS) * g


def _dot(a, b):
    return jnp.dot(a, b, preferred_element_type=F32)


def _dot_nt(a, b):
    return lax.dot_general(a, b, (((1,), (1,)), ((), ())), preferred_element_type=F32)


def _bmm(a, b):
    return jnp.einsum("gmk,gkn->gmn", a, b, preferred_element_type=F32)


def _bmm_nt(a, b):
    return jnp.einsum("gmk,gnk->gmn", a, b, preferred_element_type=F32)


def _swiglu_half(x, g_ref, wg_ref, wu_ref, wd_ref):
    h = _rms(x, g_ref[...]).astype(BF16)
    gate = _dot(h, wg_ref[...])
    up = _dot(h, wu_ref[...])
    act = (gate * jax.nn.sigmoid(gate) * up).astype(BF16)
    return x + 0.5 * _dot(act, wd_ref[...])


def _ffn_kernel(x_ref, g_ref, wg_ref, wu_ref, wd_ref, gf_ref, o_ref, *, final_norm):
    y = _swiglu_half(x_ref[...], g_ref, wg_ref, wu_ref, wd_ref)
    if final_norm:
        y = _rms(y, gf_ref[...])
    o_ref[...] = y


def ffn_half(x, l, g, wg, wu, wd, gf, *, final_norm):
    n, d = x.shape
    tm = min(TOKEN_TILE, n)
    return pl.pallas_call(
        functools.partial(_ffn_kernel, final_norm=final_norm),
        grid=(n // tm,),
        in_specs=[pl.BlockSpec((tm, d), lambda i: (i, 0)), _layer_spec(g, l), _layer_spec(wg, l),
                  _layer_spec(wu, l), _layer_spec(wd, l), _layer_spec(gf, 0)],
        out_specs=pl.BlockSpec((tm, d), lambda i: (i, 0)),
        out_shape=jax.ShapeDtypeStruct((n, d), F32),
        compiler_params=_cparams("parallel"),
        name="ffn_half",
    )(x, g, wg, wu, wd, gf)


def _proj_kernel(x_ref, g1_ref, wg_ref, wu_ref, wd_ref, g_ref, win_ref, qn_ref, wuq_ref, wuk_ref, kvn_ref,
                 cq_ref, sq_ref, ck_ref, sk_ref, x1_ref, q_ref, ckv_ref, kr_ref, pb_ref):
    x = _swiglu_half(x_ref[...], g1_ref, wg_ref, wu_ref, wd_ref)
    x1_ref[...] = x
    h = _rms(x, g_ref[...]).astype(BF16)
    proj = _dot(h, win_ref[...])
    c_q = proj[:, :Q_RANK]
    c_kv = proj[:, Q_RANK:Q_RANK + KV_RANK]
    o = Q_RANK + KV_RANK
    k_r, k_rot = proj[:, o:o + DR], proj[:, o + DR:o + 2 * DR]
    ckv_ref[...] = _rms(c_kv, kvn_ref[...])
    kr_t = (k_r * ck_ref[...] + k_rot * sk_ref[...]).T
    tw = kr_ref.shape[2]
    for s in range(kr_ref.shape[0]):
        kr_ref[s] = kr_t[:, s * tw:(s + 1) * tw]
    pb_ref[...] = proj[:, PROJ_HEAD:]
    qf = _dot(_rms(c_q, qn_ref[...]).astype(BF16), wuq_ref[...])
    nq = H_A * DN
    q_rope = (qf[:, nq:nq + H_A * DR] * cq_ref[...] + qf[:, nq + H_A * DR:] * sq_ref[...]) * (MLA_SCALE * LOG2E)
    for hd in range(H_A):
        q_lat = _dot_nt(qf[:, hd * DN:(hd + 1) * DN].astype(BF16), wuk_ref[hd]) * (MLA_SCALE * LOG2E)
        q_ref[hd, :, :KV_RANK] = q_lat.astype(BF16)
        q_ref[hd, :, KV_RANK:] = q_rope[:, hd * DR:(hd + 1) * DR].astype(BF16)


def mixer_proj(x, l, g1, wg, wu, wd, g, w_in_ext, q_norm, w_uq_ext, w_uk, kv_norm, cos_q, sin_q, cos_k, sin_k, *,
               t_stream):
    n, d = x.shape
    tm = min(FUSED_TILE, n)
    n_tab = cos_q.shape[0] // tm
    spt = max(1, tm // t_stream)
    tps = max(1, t_stream // tm)
    tok = lambda w: pl.BlockSpec((tm, w), lambda i: (i, 0))
    tab = lambda w: pl.BlockSpec((tm, w), lambda i: (i % n_tab, 0))
    kr_spec = pl.BlockSpec((spt, DR, tm // spt), lambda i: (i // tps, 0, i % tps))
    return pl.pallas_call(
        _proj_kernel,
        grid=(n // tm,),
        in_specs=[tok(d), _layer_spec(g1, l), _layer_spec(wg, l), _layer_spec(wu, l), _layer_spec(wd, l),
                  _layer_spec(g, l), _layer_spec(w_in_ext, l), _layer_spec(q_norm, l),
                  _layer_spec(w_uq_ext, l), _layer_spec(w_uk, l), _layer_spec(kv_norm, l),
                  tab(H_A * DR), tab(H_A * DR), tab(DR), tab(DR)],
        out_specs=[tok(d), pl.BlockSpec((H_A, tm, DQK), lambda i: (0, i, 0)), tok(KV_RANK), kr_spec, tok(D_SHIFT)],
        out_shape=[jax.ShapeDtypeStruct((n, d), F32), jax.ShapeDtypeStruct((H_A, n, DQK), BF16),
                   jax.ShapeDtypeStruct((n, KV_RANK), F32),
                   jax.ShapeDtypeStruct((n // t_stream, DR, t_stream), F32),
                   jax.ShapeDtypeStruct((n, D_SHIFT), F32)],
        compiler_params=_cparams("parallel"),
        name="mixer_proj",
    )(x, g1, wg, wu, wd, g, w_in_ext, q_norm, w_uq_ext, w_uk, kv_norm, cos_q, sin_q, cos_k, sin_k)


def _mla_kernel(*refs, tq, kb, n_past, t_new, has_past):
    if has_past:
        q_ref, ckv_ref, kr_ref, pckv_ref, pkr_ref, wuvt_ref, o_ref, kcat, vt = refs
    else:
        q_ref, ckv_ref, kr_ref, wuvt_ref, o_ref, kcat, vt = refs
    i = pl.program_id(1)
    t_all = n_past + t_new
    t_pad = kcat.shape[0]

    @pl.when(i == 0)
    def _():
        if t_pad > t_all:
            kcat[t_all:, :] = jnp.zeros((t_pad - t_all, DQK), BF16)
            vt[:, t_all:] = jnp.zeros((KV_RANK, t_pad - t_all), BF16)
        pieces = [(0, pckv_ref, pkr_ref, n_past)] if has_past else []
        for base, c_ref, r_ref, n in pieces + [(n_past, ckv_ref, kr_ref, t_new)]:
            kcat[base:base + n, :KV_RANK] = c_ref[0].astype(BF16)
            step_t = min(n, 512)
            for o in range(0, n, step_t):
                vt[:, base + o:base + o + step_t] = c_ref[0, o:o + step_t, :].T.astype(BF16)
                kcat[base + o:base + o + step_t, KV_RANK:] = r_ref[0, :, o:o + step_t].T.astype(BF16)

    rows = H_A * tq
    q = q_ref[...].reshape(rows, DQK)
    t_in = lax.broadcasted_iota(jnp.int32, (1, rows), 1) & (tq - 1)
    limit = ((n_past + i * tq + t_in) // CHUNK + 1) * CHUNK
    n_blocks = (n_past + (i + 1) * tq + kb - 1) // kb

    def scores(j):
        return _dot_nt(kcat[pl.ds(pl.multiple_of(j * kb, kb), kb), :], q)

    def update(j, carry, st, masked):
        m, l, acc = carry
        if masked:
            k_pos = j * kb + lax.broadcasted_iota(jnp.int32, (kb, 1), 0)
            st = jnp.where(k_pos < limit, st, NEG_INF)
        m_new = jnp.maximum(m, jnp.max(st, axis=0, keepdims=True))
        alpha = jnp.exp2(m - m_new)
        p = jnp.exp2(st - m_new)
        l = alpha * l + jnp.sum(p, axis=0, keepdims=True)
        acc = alpha * acc + _dot(vt[:, pl.ds(pl.multiple_of(j * kb, kb), kb)], p.astype(BF16))
        return m_new, l, acc

    def pair(jj, carry):
        st0, st1 = scores(2 * jj), scores(2 * jj + 1)
        return update(2 * jj + 1, update(2 * jj, carry, st0, False), st1, False)

    def single(j, carry):
        return update(j, carry, scores(j), False)

    m0 = jnp.full((1, rows), NEG_INF, F32)
    l0 = jnp.zeros((1, rows), F32)
    a0 = jnp.zeros((KV_RANK, rows), F32)
    n_pairs = (n_blocks - 1) // 2
    carry = lax.fori_loop(0, n_pairs, pair, (m0, l0, a0))
    carry = lax.fori_loop(2 * n_pairs, n_blocks - 1, single, carry)
    _, l, acc = update(n_blocks - 1, carry, scores(n_blocks - 1), True)
    o_lat = (acc / l).astype(BF16)
    outs = [_dot(wuvt_ref[hd], o_lat[:, hd * tq:(hd + 1) * tq]) for hd in range(H_A)]
    o_ref[0] = jnp.concatenate(outs, axis=0).T.astype(o_ref.dtype)


def mla_attention(q, ckv, kr, l, past_ckv, past_kr, w_uv_t, *, tq, kb):
    b, t, _ = ckv.shape
    has_past = past_ckv is not None
    n_past = past_ckv.shape[2] if has_past else 0
    t_pad = -(-(n_past + t) // kb) * kb
    nq = t // tq
    for i in range(nq):
        first_limit = ((n_past + i * tq) // CHUNK + 1) * CHUNK
        assert first_limit // kb >= -(-(n_past + (i + 1) * tq) // kb) - 1, (tq, kb, n_past, i)
    stream = lambda n, w: pl.BlockSpec((1, n, w), lambda bi, i: (bi, 0, 0))
    past_c = pl.BlockSpec((None, 1, n_past, KV_RANK), lambda bi, i: (l, bi, 0, 0))
    past_r = pl.BlockSpec((None, 1, DR, n_past), lambda bi, i: (l, bi, 0, 0))
    in_specs = [pl.BlockSpec((H_A, tq, DQK), lambda bi, i: (0, bi * nq + i, 0)), stream(t, KV_RANK), stream(DR, t)]
    args = [q, ckv, kr]
    if has_past:
        in_specs += [past_c, past_r]
        args += [past_ckv, past_kr]
    in_specs.append(_layer_spec(w_uv_t, l))
    args.append(w_uv_t)
    return pl.pallas_call(
        functools.partial(_mla_kernel, tq=tq, kb=kb, n_past=n_past, t_new=t, has_past=has_past),
        grid=(b, nq),
        in_specs=in_specs,
        out_specs=pl.BlockSpec((1, tq, H_A * DV), lambda bi, i: (bi, i, 0)),
        out_shape=jax.ShapeDtypeStruct((b, t, H_A * DV), BF16),
        scratch_shapes=[pltpu.VMEM((t_pad, DQK), BF16), pltpu.VMEM((KV_RANK, t_pad), BF16)],
        compiler_params=_cparams("parallel", "arbitrary"),
        name="mla_attention",
    )(*args)


def _split3(x):
    hi = x.astype(BF16)
    r1 = x - hi.astype(F32)
    mid = r1.astype(BF16)
    lo = (r1 - mid.astype(F32)).astype(BF16)
    return hi, mid, lo


def _wkv_kernel(pb_ref, sh0_ref, s0_ref, mu_ref, w0_ref, wup_ref, a0_ref, aup_ref, gup_ref, kk_ref, ka_ref,
                rk_ref, gng_ref, gnb_ref, seg_ref, y_ref, so_ref, st_scr, prev_scr, *, sb):
    c = pl.program_id(1)
    nb = sb * H_B
    n = sb * CHUNK

    @pl.when(c == 0)
    def _():
        st_scr[...] = jnp.swapaxes(s0_ref[...].reshape(nb, N_B, N_B), 1, 2)
        prev_scr[...] = sh0_ref[...]

    row = lax.broadcasted_iota(jnp.int32, (CHUNK, 1), 0)
    parts = []
    for s in range(sb):
        pb = pb_ref[s]
        prev = jnp.where(row == 0, prev_scr[s], pltpu.roll(pb, shift=1, axis=0))
        prev_scr[s] = pb[CHUNK - 1:, :]
        parts.append(pb + mu_ref[...] * (prev - pb))
    xs = jnp.concatenate(parts, axis=0)
    r = xs[:, :D_B]
    k = xs[:, D_B:2 * D_B]
    v = xs[:, 2 * D_B:3 * D_B]
    o = 3 * D_B
    xw, xa, xg = xs[:, o:o + W_RANK], xs[:, o + W_RANK:o + W_RANK + A_RANK], xs[:, o + W_RANK + A_RANK:]
    z = w0_ref[...] + _dot(jnp.tanh(xw).astype(BF16), wup_ref[...])
    logw = -math.exp(-0.5) * jax.nn.sigmoid(z)
    gate = jax.nn.sigmoid(a0_ref[...] + _dot(xa.astype(BF16), aup_ref[...]))
    g = _dot(jax.nn.sigmoid(xg).astype(BF16), gup_ref[...])
    kkf = k * kk_ref[...]
    k2 = k * (1.0 + (gate - 1.0) * ka_ref[...])
    ri = lax.broadcasted_iota(jnp.int32, (n, 1), 0)
    ci = lax.broadcasted_iota(jnp.int32, (1, n), 1)
    same_stream = jnp.right_shift(ri, 6) == jnp.right_shift(ci, 6)
    ones_tri = jnp.where(ri >= ci, jnp.where(same_stream, 1.0, 0.0), 0.0).astype(BF16)
    cs = sum(_dot(ones_tri, piece) for piece in _split3(logw))
    p_in = jnp.exp(cs)
    p_prev = jnp.exp(cs - logw)
    p_inv = jnp.exp(-cs)

    ones_blk = seg_ref[...]

    def seg_sum(x):
        hi = x.astype(BF16)
        lo = (x - hi.astype(F32)).astype(BF16)
        return _dot(hi, ones_blk) + _dot(lo, ones_blk)

    def heads3(x):
        return jnp.stack([x[s * CHUNK:(s + 1) * CHUNK, h * N_B:(h + 1) * N_B]
                          for s in range(sb) for h in range(H_B)], axis=0)

    kk = kkf * lax.rsqrt(seg_sum(kkf * kkf) + 1e-12)
    bonus = seg_sum(r * k2 * rk_ref[...]) * v
    at32, rt32 = heads3(-kk * p_prev), heads3(r * p_in)
    kt32, bt32 = heads3(k2 * p_inv), heads3(kk * gate * p_inv)
    v3 = heads3(v)
    pend3 = jnp.stack([p_in[(s + 1) * CHUNK - 1:(s + 1) * CHUNK, h * N_B:(h + 1) * N_B]
                       for s in range(sb) for h in range(H_B)], axis=0)
    at, rt = at32.astype(BF16), rt32.astype(BF16)
    bk = jnp.concatenate([bt32, kt32], axis=1).astype(BF16)
    bhkh_t = jnp.swapaxes(jnp.concatenate([bt32 * pend3, kt32 * pend3], axis=1), 1, 2).astype(BF16)
    vb = v3.astype(BF16)

    tt = lax.broadcasted_iota(jnp.int32, (1, CHUNK, 2 * CHUNK), 1)
    ss = lax.broadcasted_iota(jnp.int32, (1, CHUNK, 2 * CHUNK), 2) & (CHUNK - 1)
    a_top = jnp.where(tt > ss, _bmm_nt(at, bk), 0.0)
    a_bot = jnp.where(tt >= ss, _bmm_nt(rt, bk), 0.0).astype(BF16)
    zero_v = jnp.concatenate([jnp.zeros((nb, CHUNK, N_B), BF16), vb], axis=1)
    zz = jnp.concatenate([at32, _bmm(a_top.astype(BF16), zero_v)], axis=-1)
    x = a_top[:, :, :CHUNK]
    for lvl in range(6):
        xb = x.astype(BF16)
        if lvl < 5:
            res = _bmm(xb, jnp.concatenate([zz.astype(BF16), xb], axis=-1))
            zz = zz + res[:, :, :2 * N_B]
            x = res[:, :, 2 * N_B:]
        else:
            zz = zz + _bmm(xb, zz.astype(BF16))
    w_m, u0 = zz[:, :, :N_B].astype(BF16), zz[:, :, N_B:].astype(BF16)
    res2 = _bmm(jnp.concatenate([a_bot[:, :, :CHUNK], bhkh_t[:, :, :CHUNK]], axis=1), w_m)
    q_hat = rt32 + res2[:, :CHUNK]
    eye = lax.broadcasted_iota(jnp.int32, (1, N_B, N_B), 1) == lax.broadcasted_iota(jnp.int32, (1, N_B, N_B), 2)
    m_t = jnp.where(eye, pend3, 0.0) + res2[:, CHUNK:]
    lhs = jnp.concatenate([jnp.concatenate([a_bot, q_hat.astype(BF16)], axis=-1),
                           jnp.concatenate([bhkh_t, m_t.astype(BF16)], axis=-1)], axis=1)
    rhs = jnp.concatenate([u0, vb, st_scr[...].astype(BF16)], axis=1)
    res3 = _bmm(lhs, rhs)
    y3 = res3[:, :CHUNK]
    st_scr[...] = res3[:, CHUNK:]

    y = jnp.concatenate([jnp.concatenate([y3[s * H_B + h] for h in range(H_B)], axis=-1) for s in range(sb)], axis=0)
    d = y - seg_sum(y) * (1.0 / N_B)
    var = seg_sum(d * d) * (1.0 / N_B)
    y = (d * lax.rsqrt(var + GN_EPS) * gng_ref[...] + gnb_ref[...] + bonus) * g
    for s in range(sb):
        y_ref[s] = y[s * CHUNK:(s + 1) * CHUNK].astype(y_ref.dtype)

    @pl.when(c == pl.num_programs(1) - 1)
    def _():
        so_ref[...] = jnp.swapaxes(st_scr[...], 1, 2).reshape(sb, H_B, N_B, N_B)


def wkv_group(pb, l, ls, shift0, s0, mu, w0, w_up, a0, a_up, g_up, k_k, k_a, r_k, gn_g, gn_b):
    b, t, _ = pb.shape
    sb = min(WKV_STREAMS, b)
    small = [mu, w0, w_up, a0, a_up, g_up, k_k, k_a, r_k, gn_g, gn_b]
    head_of = jnp.arange(D_B, dtype=jnp.int32) // N_B
    same_head = (head_of[:, None] == head_of[None, :]).astype(BF16)
    return pl.pallas_call(
        functools.partial(_wkv_kernel, sb=sb),
        grid=(b // sb, t // CHUNK),
        in_specs=[pl.BlockSpec((sb, CHUNK, D_SHIFT), lambda bi, c: (bi, c, 0)),
                  pl.BlockSpec((None, sb, 1, D_SHIFT), lambda bi, c: (ls, bi, 0, 0)),
                  pl.BlockSpec((None, sb, H_B, N_B, N_B), lambda bi, c: (ls, bi, 0, 0, 0))]
        + [_layer_spec(a, l) for a in small]
        + [pl.BlockSpec((D_B, D_B), lambda bi, c: (0, 0), pipeline_mode=pl.Buffered(1))],
        out_specs=[pl.BlockSpec((sb, CHUNK, D_B), lambda bi, c: (bi, c, 0)),
                   pl.BlockSpec((sb, H_B, N_B, N_B), lambda bi, c: (bi, 0, 0, 0))],
        out_shape=[jax.ShapeDtypeStruct((b, t, D_B), BF16), jax.ShapeDtypeStruct((b, H_B, N_B, N_B), F32)],
        scratch_shapes=[pltpu.VMEM((sb * H_B, N_B, N_B), F32), pltpu.VMEM((sb, 1, D_SHIFT), F32)],
        compiler_params=_cparams("parallel", "arbitrary"),
        name="wkv_group",
    )(pb, shift0, s0, *small, same_head)


def _out_kernel(x_ref, ya_ref, yb_ref, wo_ref, g_ref, wq_ref, mk_ref, mv_ref, wmo_ref, o_ref, *, spt):
    x = x_ref[...] + _dot(ya_ref[...], wo_ref[:H_A * DV, :]) + _dot(yb_ref[...], wo_ref[H_A * DV:, :])
    q = _dot(_rms(x, g_ref[...]).astype(BF16), wq_ref[...])
    ts = x.shape[0] // spt
    halves = lambda hd: [pl.ds(j * MEM_HEADS + hd, N_MEM, stride=MEM_ROWS) for j in range(MEM_LANE_HALVES)]
    pairs = [(s, hd) for s in range(spt) for hd in range(MEM_HEADS)]
    scores = []
    for s, hd in pairs:
        qh = q[s * ts:(s + 1) * ts, hd * MEM_HD:(hd + 1) * MEM_HD].astype(BF16)
        scores.append(sum(_dot_nt(qh[:, j * 128:(j + 1) * 128], mk_ref[s, rws, :].astype(BF16))
                          for j, rws in enumerate(halves(hd))) / math.sqrt(MEM_HD))
    probs = []
    for sc in scores:
        sc = jnp.exp(sc - jnp.max(sc, axis=-1, keepdims=True))
        probs.append((sc / jnp.sum(sc, axis=-1, keepdims=True)).astype(BF16))
    outs = {}
    for (s, hd), p in zip(pairs, probs):
        outs[s, hd] = [_dot(p, mv_ref[s, rws, :].astype(BF16)) for rws in halves(hd)]
    rows = []
    for s in range(spt):
        rows.append(jnp.concatenate([o for hd in range(MEM_HEADS) for o in outs[s, hd]], axis=-1))
    att = rows[0] if spt == 1 else jnp.concatenate(rows, axis=0)
    o_ref[...] = x + _dot(att.astype(BF16), wmo_ref[...])


def mixer_out_xattn(x, ya, yb, l, w_out, g, w_mq, lm, mk, mv, w_mo, *, t_stream):
    n, d = x.shape
    tm = min(2 * TOKEN_TILE if t_stream >= 2 * TOKEN_TILE else TOKEN_TILE, n)
    spt = max(1, tm // t_stream)
    tps = max(1, t_stream // tm)
    tok = lambda w: pl.BlockSpec((tm, w), lambda i: (i, 0))
    mem = pl.BlockSpec((None, spt, N_MEM * MEM_ROWS, 128), lambda i: (lm, i // tps, 0, 0))
    return pl.pallas_call(
        functools.partial(_out_kernel, spt=spt),
        grid=(n // tm,),
        in_specs=[tok(d), tok(H_A * DV), tok(D_B), _layer_spec(w_out, l), _layer_spec(g, l),
                  _layer_spec(w_mq, l), mem, mem, _layer_spec(w_mo, l)],
        out_specs=tok(d),
        out_shape=jax.ShapeDtypeStruct((n, d), F32),
        compiler_params=_cparams("parallel"),
        name="mixer_out_xattn",
    )(x, ya, yb, w_out, g, w_mq, mk, mv, w_mo)


def _memkv_kernel(m_ref, g_ref, wk_ref, wv_ref, k_ref, v_ref):
    tm = m_ref.shape[0]
    m = _rms(m_ref[...], g_ref[...]).astype(BF16)
    for w_ref, o_ref in ((wk_ref, k_ref), (wv_ref, v_ref)):
        kv = _dot(m, w_ref[...])
        for hd in range(MEM_HEADS):
            for j in range(MEM_LANE_HALVES):
                col = hd * MEM_HD + j * 128
                o_ref[pl.ds(j * MEM_HEADS + hd, tm, stride=MEM_ROWS), :] = kv[:, col:col + 128]


def mem_kv(mem, l, g, w_mk, w_mv):
    n, d = mem.shape
    tm = min(TOKEN_TILE, n)
    rows = pl.BlockSpec((tm * MEM_ROWS, 128), lambda i: (i, 0))
    return pl.pallas_call(
        _memkv_kernel,
        grid=(n // tm,),
        in_specs=[pl.BlockSpec((tm, d), lambda i: (i, 0)), _layer_spec(g, l), _layer_spec(w_mk, l),
                  _layer_spec(w_mv, l)],
        out_specs=[rows, rows],
        out_shape=[jax.ShapeDtypeStruct((n * MEM_ROWS, 128), F32)] * 2,
        compiler_params=_cparams("parallel"),
        name="mem_kv",
    )(mem, g, w_mk, w_mv)


def _rot_half_cols(w):
    half = DR // 2
    shp = w.shape
    w = w.reshape(shp[:-1] + (shp[-1] // DR, 2, half))
    return jnp.concatenate([-w[..., 1:, :], w[..., :1, :]], axis=-2).reshape(shp)


def _mem_rows(a):
    lead = a.shape[:-3]
    a = a.reshape(lead + (N_MEM, MEM_HEADS, MEM_LANE_HALVES, 128))
    return jnp.swapaxes(a, -2, -3).reshape(lead + (N_MEM * MEM_ROWS, 128))


def _mem_heads(a):
    lead = a.shape[:-2]
    a = a.reshape(lead + (N_MEM, MEM_LANE_HALVES, MEM_HEADS, 128))
    return jnp.swapaxes(a, -2, -3).reshape(lead + (N_MEM, MEM_HEADS, MEM_HD))


def _rope_tables(pos, reps):
    half = DR // 2
    inv = ROPE_THETA ** (-jnp.arange(half, dtype=F32) / half)
    ang = pos.astype(F32)[:, None] * inv[None, :]
    cos = jnp.tile(jnp.cos(ang), (1, 2 * reps))
    sin = jnp.tile(jnp.sin(ang), (1, 2 * reps))
    return cos, sin


def _group_tables(pos, n_tokens):
    rows = max(pos.shape[0], min(FUSED_TILE, n_tokens))
    pos = jnp.tile(pos, rows // pos.shape[0])
    return _rope_tables(pos, H_A) + _rope_tables(pos, 1)


def _layer_group(x, tabs, lw, l, b, t, past_ckv, past_kr, ls, wkv0, shift0, lm, mk, mv, *, tq, kb, final_norm):
    x, q, ckv, kr, pb = mixer_proj(x, l, lw["ffn1_norm"], lw["ffn1_wg"], lw["ffn1_wu"], lw["ffn1_wd"],
                                   lw["mix_norm"], lw["w_in_ext"], lw["q_norm"], lw["w_uq_ext"], lw["w_uk"],
                                   lw["kv_norm"], *tabs, t_stream=t)
    ckv3 = ckv.reshape(b, t, KV_RANK)
    ya = mla_attention(q, ckv3, kr, l, past_ckv, past_kr, lw["w_uv_t"], tq=tq, kb=kb)
    pb3 = pb.reshape(b, t, D_SHIFT)
    yb, wkv = wkv_group(pb3, l, ls, shift0, wkv0, *(lw[n] for n in (
        "shift_mu", "w0", "w_up", "a0", "a_up", "g_up", "k_k", "k_a", "r_k", "gn_gain", "gn_bias")))
    x = mixer_out_xattn(x, ya.reshape(b * t, H_A * DV), yb.reshape(b * t, D_B), l, lw["w_out"], lw["xattn_norm"],
                        lw["w_mq"], lm, mk, mv, lw["w_mo"], t_stream=t)
    x = ffn_half(x, l, lw["ffn2_norm"], lw["ffn2_wg"], lw["ffn2_wu"], lw["ffn2_wd"], lw["final_norm"],
                 final_norm=final_norm)
    return x, ckv3, kr, wkv, pb3[:, -1:, :]


def kernel(x_prompt, x_sample, mem_prompt, cache_ckv, cache_krope, cache_mem_k, cache_mem_v, state_wkv, state_shift, ffn1_norm, ffn1_w_gate, ffn1_w_up, ffn1_w_down, mix_norm, w_in, q_norm, w_uq, kv_norm, w_uk, w_uv, shift_mu, w0, w_up, a0, a_up, g_up, k_k, k_a, r_k, gn_gain, gn_bias, w_out, xattn_norm, mem_kv_norm, w_mq, w_mk, w_mv, w_mo, ffn2_norm, ffn2_w_gate, ffn2_w_up, ffn2_w_down, final_norm):
    depth = w_in.shape[0]
    b_p, t_p, d = x_prompt.shape
    b_s, t_s, _ = x_sample.shape
    n_past = cache_ckv.shape[2]
    bf = lambda a: a.astype(BF16)
    vec = lambda a: a.reshape(depth, 1, -1)

    o = Q_RANK + KV_RANK
    w_kr = w_in[..., o:o + DR]
    w_in_ext = jnp.concatenate(
        [w_in[..., :o + DR], _rot_half_cols(w_kr), jnp.zeros(w_in.shape[:2] + (PROJ_HEAD - o - 2 * DR,), w_in.dtype),
         w_in[..., o + DR:]], axis=-1)
    uq = w_uq.reshape(depth, Q_RANK, H_A, DN + DR)
    uq_nope = uq[..., :DN].reshape(depth, Q_RANK, H_A * DN)
    uq_rope = uq[..., DN:].reshape(depth, Q_RANK, H_A * DR)
    w_uq_ext = jnp.concatenate([uq_nope, uq_rope, _rot_half_cols(uq_rope)], axis=-1)
    lw = dict(
        ffn1_norm=vec(ffn1_norm), ffn1_wg=bf(ffn1_w_gate), ffn1_wu=bf(ffn1_w_up), ffn1_wd=bf(ffn1_w_down),
        mix_norm=vec(mix_norm), w_in_ext=bf(w_in_ext), q_norm=vec(q_norm), w_uq_ext=bf(w_uq_ext),
        kv_norm=vec(kv_norm), w_uk=bf(w_uk), w_uv_t=bf(jnp.swapaxes(w_uv, -1, -2)), shift_mu=vec(shift_mu),
        w0=vec(w0), w_up=bf(w_up), a0=vec(a0), a_up=bf(a_up), g_up=bf(g_up), k_k=vec(k_k), k_a=vec(k_a),
        r_k=vec(r_k), gn_gain=vec(gn_gain), gn_bias=vec(gn_bias), w_out=bf(w_out), xattn_norm=vec(xattn_norm),
        w_mq=bf(w_mq), w_mo=bf(w_mo), ffn2_norm=vec(ffn2_norm), ffn2_wg=bf(ffn2_w_gate), ffn2_wu=bf(ffn2_w_up),
        ffn2_wd=bf(ffn2_w_down), final_norm=final_norm.reshape(1, 1, d))
    mem_norm, w_mk_b, w_mv_b = vec(mem_kv_norm), bf(w_mk), bf(w_mv)

    tabs_p = _group_tables(jnp.arange(t_p, dtype=jnp.int32), b_p * t_p)
    tabs_s = _group_tables(n_past + jnp.arange(t_s, dtype=jnp.int32), b_s * t_s)
    wkv_zero = jnp.zeros((1, b_p, H_B, N_B, N_B), x_prompt.dtype)
    shift_zero = jnp.zeros((1, b_p, 1, D_SHIFT), x_prompt.dtype)
    krope_t = jnp.swapaxes(cache_krope, -1, -2)
    mem_k_s, mem_v_s = _mem_rows(cache_mem_k), _mem_rows(cache_mem_v)
    kb_s = -(-(n_past + t_s) // (3 * 128)) * 128

    xp = x_prompt.reshape(b_p * t_p, d)
    xs = x_sample.reshape(b_s * t_s, d)
    mem2 = mem_prompt.reshape(b_p * N_MEM, d)
    outs_p = [[] for _ in range(6)]
    outs_s = [[] for _ in range(4)]
    for l in range(depth):
        last = l == depth - 1
        mk, mv = (a.reshape(1, b_p, N_MEM * MEM_ROWS, 128) for a in mem_kv(mem2, l, mem_norm, w_mk_b, w_mv_b))
        xp, c1, k1, s1, h1 = _layer_group(xp, tabs_p, lw, l, b_p, t_p, None, None, 0, wkv_zero, shift_zero, 0,
                                          mk, mv, tq=min(512, t_p), kb=512, final_norm=last)
        for acc, val in zip(outs_p, (c1, k1, mk[0], mv[0], s1, h1)):
            acc.append(val)
        xs, c2, k2, s2, h2 = _layer_group(xs, tabs_s, lw, l, b_s, t_s, cache_ckv, krope_t, l, state_wkv,
                                          state_shift, l, mem_k_s, mem_v_s, tq=t_s, kb=kb_s, final_norm=last)
        for acc, val in zip(outs_s, (c2, k2, s2, h2)):
            acc.append(val)
    stack = lambda seq: jnp.stack(seq)
    rope_keys = lambda seq: jnp.swapaxes(jnp.stack(seq), -1, -2)
    return (xp.reshape(b_p, t_p, d), xs.reshape(b_s, t_s, d),
            stack(outs_p[0]), rope_keys(outs_p[1]), _mem_heads(stack(outs_p[2])), _mem_heads(stack(outs_p[3])),
            stack(outs_p[4]), stack(outs_p[5]),
            stack(outs_s[0]), rope_keys(outs_s[1]), stack(outs_s[2]), stack(outs_s[3]))
```
